```python
import math
import jax, jax.numpy as jnp
from jax import lax
import numpy as np

D_MODEL = 4096
BATCH = 4
SEQ = 4096
DEPTH = 2
DEC_BATCH = 1
DEC_SEQ = 16384
PAST_LEN = 128

HEAD_DIM = 128
N_HEADS_A = 16
N_HEADS_B = 16
N_KV_B = 4
WIDTH_A = N_HEADS_A * HEAD_DIM
WIDTH_B = N_HEADS_B * HEAD_DIM
KV_WIDTH_B = N_KV_B * HEAD_DIM
MIX_WIDTH = WIDTH_A + WIDTH_B
IN_WIDTH = 3 * WIDTH_A + WIDTH_B + 2 * KV_WIDTH_B
D_FF = 4 * D_MODEL
DILATED_PAIRS = ((128, 1), (512, 4), (2048, 16))
PARTIAL_ROT_DIM = HEAD_DIM // 4
ROPE_THETA = 500000.0
AXIAL_THETA = 10000.0
GRID_W = 64
Q_BLOCK = 128
DEEPNORM_ALPHA = (2 * DEPTH) ** 0.25
DEEPNORM_BETA = (8 * DEPTH) ** -0.25
LN_EPS = 1e-5
RMS_EPS = 1e-6
NEG_BIG = -1e30

kernel_name = 'hybrid_dilated_axial_gqa_encoder'


def _layernorm(x, g, b):
    xf = x.astype(jnp.float32)
    mu = xf.mean(-1, keepdims=True)
    var = jnp.square(xf - mu).mean(-1, keepdims=True)
    return ((xf - mu) * lax.rsqrt(var + LN_EPS) * g + b).astype(x.dtype)


def _rmsnorm(x, g):
    xf = x.astype(jnp.float32)
    return (xf * lax.rsqrt(jnp.square(xf).mean(-1, keepdims=True) + RMS_EPS) * g).astype(x.dtype)


def _rope(x, pos, theta):
    r = x.shape[-1]
    inv = theta ** (-jnp.arange(0, r, 2, dtype=jnp.float32) / r)
    ang = pos[:, None] * inv[None, :]
    cos = jnp.cos(ang)[None, :, None, :]
    sin = jnp.sin(ang)[None, :, None, :]
    xf = x.astype(jnp.float32)
    x1, x2 = xf[..., : r // 2], xf[..., r // 2:]
    return jnp.concatenate([x1 * cos - x2 * sin, x2 * cos + x1 * sin], -1).astype(x.dtype)


def _partial_rope(x, pos):
    return jnp.concatenate([_rope(x[..., :PARTIAL_ROT_DIM], pos, ROPE_THETA), x[..., PARTIAL_ROT_DIM:]], -1)


def _dilated_branch(q, k, v, window, dilation):
    B, S, H, E = q.shape
    half = window // (2 * dilation)
    L = S // dilation
    nb = -(-L // half)
    Lp = nb * half

    def blocks(t):
        t = t.reshape(B, L, dilation, H, E).transpose(0, 2, 1, 3, 4)
        t = jnp.pad(t, ((0, 0), (0, 0), (0, Lp - L), (0, 0), (0, 0)))
        return t.reshape(B, dilation, nb, half, H, E)

    def neighbours(t):
        tp = jnp.pad(t, ((0, 0), (0, 0), (1, 1), (0, 0), (0, 0), (0, 0)))
        return jnp.concatenate([tp[:, :, :-2], tp[:, :, 1:-1], tp[:, :, 2:]], axis=3)

    qb = blocks(q)
    kb = neighbours(blocks(k))
    vb = neighbours(blocks(v))
    qi = jnp.arange(half)[:, None]
    kj = jnp.arange(3 * half)[None, :]
    blk = jnp.arange(nb)[:, None, None]
    key_idx = (blk - 1) * half + kj
    delta = kj - half - qi
    valid = (jnp.abs(delta) <= half) & (key_idx >= 0) & (key_idx < L)
    s = jnp.einsum('brnqhe,brnkhe->brnhqk', qb, kb, preferred_element_type=jnp.float32) * (E ** -0.5)
    s = jnp.where(valid[None, None, :, None], s, NEG_BIG)
    m = s.max(-1)
    p = jnp.exp(s - m[..., None])
    den = p.sum(-1)
    o = jnp.einsum('brnhqk,brnkhe->brnqhe', p.astype(v.dtype), vb, preferred_element_type=jnp.float32)
    o = o / jnp.swapaxes(den, 3, 4)[..., None]

    def unblocks(t):
        f = t.shape[-1]
        t = t.reshape(B, dilation, Lp, H, f)[:, :, :L]
        return t.transpose(0, 2, 1, 3, 4).reshape(B, S, H, f)

    m_t = unblocks(jnp.swapaxes(m, 3, 4)[..., None])[..., 0]
    den_t = unblocks(jnp.swapaxes(den, 3, 4)[..., None])[..., 0]
    return unblocks(o), m_t, den_t


def _mixer_dilated(q, k, v, pos):
    q = _partial_rope(q, pos)
    k = _partial_rope(k, pos)
    outs = [_dilated_branch(q, k, v, w, d) for (w, d) in DILATED_PAIRS]
    m_all = outs[0][1]
    for _, m_i, _ in outs[1:]:
        m_all = jnp.maximum(m_all, m_i)
    num = 0.0
    tot = 0.0
    for o_i, m_i, den_i in outs:
        w_i = den_i * jnp.exp(m_i - m_all)
        num = num + w_i[..., None] * o_i
        tot = tot + w_i
    return (num / tot[..., None]).astype(q.dtype)


def _mixer_axial_gqa(q, k, v, g_q, g_k, row, col):
    q = _rmsnorm(q, g_q)
    k = _rmsnorm(k, g_k)
    hd = HEAD_DIM // 2
    q = jnp.concatenate([_rope(q[..., :hd], row, AXIAL_THETA), _rope(q[..., hd:], col, AXIAL_THETA)], -1)
    k = jnp.concatenate([_rope(k[..., :hd], row, AXIAL_THETA), _rope(k[..., hd:], col, AXIAL_THETA)], -1)
    B, S, Hq, E = q.shape
    G = Hq // N_KV_B
    nqb = S // Q_BLOCK
    qb = (q * (E ** -0.5)).reshape(B, nqb, Q_BLOCK, N_KV_B, G, E).transpose(1, 0, 2, 3, 4, 5)

    def attend(qblk):
        s = jnp.einsum('bqhge,bshe->bhgqs', qblk, k, preferred_element_type=jnp.float32)
        p = jax.nn.softmax(s, axis=-1)
        return jnp.einsum('bhgqs,bshe->bqhge', p.astype(v.dtype), v)

    ob = lax.map(attend, qb)
    return ob.transpose(1, 0, 2, 3, 4, 5).reshape(B, S, Hq, E)


def _encoder_layer(x, c, w_ada, b_ada, w_in, g_q, g_k, g_out_a, g_out_b, w_out,
                   ln1_g, ln1_b, w_up, w_down, ln2_g, ln2_b):
    B, S, _ = x.shape
    rows = S // GRID_W
    pos = jnp.arange(S, dtype=jnp.float32)
    row = jnp.repeat(jnp.arange(rows, dtype=jnp.float32), GRID_W)
    col = jnp.tile(jnp.arange(GRID_W, dtype=jnp.float32), rows)

    mod = jnp.einsum('bd,de->be', jax.nn.silu(c), w_ada) + b_ada
    sh1, sc1, g1, sh2, sc2, g2 = [t[:, None, :] for t in jnp.split(mod, 6, axis=-1)]

    h = x * (1 + sc1) + sh1
    proj = jnp.einsum('bsd,de->bse', h, w_in)
    cuts = [WIDTH_A, 2 * WIDTH_A, 3 * WIDTH_A, 3 * WIDTH_A + WIDTH_B, 3 * WIDTH_A + WIDTH_B + KV_WIDTH_B]
    qa, ka, va, qb, kb, vb = jnp.split(proj, cuts, axis=-1)
    qa = qa.reshape(B, S, N_HEADS_A, HEAD_DIM)
    ka = ka.reshape(B, S, N_HEADS_A, HEAD_DIM)
    va = va.reshape(B, S, N_HEADS_A, HEAD_DIM)
    qb = qb.reshape(B, S, N_HEADS_B, HEAD_DIM)
    kb = kb.reshape(B, S, N_KV_B, HEAD_DIM)
    vb = vb.reshape(B, S, N_KV_B, HEAD_DIM)
    o_a = _mixer_dilated(qa, ka, va, pos).reshape(B, S, WIDTH_A)
    o_b = _mixer_axial_gqa(qb, kb, vb, g_q, g_k, row, col).reshape(B, S, WIDTH_B)
    mixed = jnp.concatenate([_rmsnorm(o_a, g_out_a), _rmsnorm(o_b, g_out_b)], -1)
    attn = jnp.einsum('bse,ed->bsd', mixed, w_out)
    x = _layernorm(DEEPNORM_ALPHA * x + g1 * attn, ln1_g, ln1_b)

    h = x * (1 + sc2) + sh2
    f = jnp.einsum('bsf,fd->bsd', jnp.square(jax.nn.relu(jnp.einsum('bsd,df->bsf', h, w_up))), w_down)
    x = _layernorm(DEEPNORM_ALPHA * x + g2 * f, ln2_g, ln2_b)
    return x


def setup_inputs(seed: int = 0) -> dict:
    key = jax.random.key(seed)
    ks = jax.random.split(key, 20)
    nrm = jax.random.normal
    f32 = jnp.float32
    return {
        'x_prompt': nrm(ks[0], (BATCH, SEQ, D_MODEL), f32),
        'x_sample': nrm(ks[1], (DEC_BATCH, DEC_SEQ, D_MODEL), f32),
        'c_prompt': nrm(ks[2], (BATCH, D_MODEL), f32),
        'c_sample': nrm(ks[3], (DEC_BATCH, D_MODEL), f32),
        'w_ada': nrm(ks[4], (DEPTH, D_MODEL, 6 * D_MODEL), f32) * D_MODEL ** -0.5,
        'b_ada': 0.02 * nrm(ks[5], (DEPTH, 6 * D_MODEL), f32),
        'w_in': nrm(ks[6], (DEPTH, D_MODEL, IN_WIDTH), f32) * D_MODEL ** -0.5,
        'g_q': 1.0 + 0.02 * nrm(ks[7], (DEPTH, HEAD_DIM), f32),
        'g_k': 1.0 + 0.02 * nrm(ks[8], (DEPTH, HEAD_DIM), f32),
        'g_out_a': 1.0 + 0.02 * nrm(ks[9], (DEPTH, WIDTH_A), f32),
        'g_out_b': 1.0 + 0.02 * nrm(ks[10], (DEPTH, WIDTH_B), f32),
        'w_out': nrm(ks[11], (DEPTH, MIX_WIDTH, D_MODEL), f32) * (MIX_WIDTH ** -0.5 * DEEPNORM_BETA),
        'ln1_g': 1.0 + 0.02 * nrm(ks[12], (DEPTH, D_MODEL), f32),
        'ln1_b': 0.02 * nrm(ks[13], (DEPTH, D_MODEL), f32),
        'w_up': nrm(ks[14], (DEPTH, D_MODEL, D_FF), f32) * D_MODEL ** -0.5,
        'w_down': nrm(ks[15], (DEPTH, D_FF, D_MODEL), f32) * (D_FF ** -0.5 * DEEPNORM_BETA),
        'ln2_g': 1.0 + 0.02 * nrm(ks[16], (DEPTH, D_MODEL), f32),
        'ln2_b': 0.02 * nrm(ks[17], (DEPTH, D_MODEL), f32),
    }


def reference(x_prompt, x_sample, c_prompt, c_sample, w_ada, b_ada, w_in, g_q, g_k,
              g_out_a, g_out_b, w_out, ln1_g, ln1_b, w_up, w_down, ln2_g, ln2_b):
    def run(x, c):
        for l in range(DEPTH):
            x = _encoder_layer(x, c, w_ada[l], b_ada[l], w_in[l], g_q[l], g_k[l],
                               g_out_a[l], g_out_b[l], w_out[l], ln1_g[l], ln1_b[l],
                               w_up[l], w_down[l], ln2_g[l], ln2_b[l])
        return x

    y_prompt = run(x_prompt, c_prompt)
    y_sample = run(x_sample, c_sample)
    return (y_prompt, y_sample)
```

```python
import functools

import numpy as np
import jax
import jax.numpy as jnp
from jax import lax
from jax.experimental import pallas as pl
from jax.experimental.pallas import tpu as pltpu

HEAD_DIM = 128
N_HEADS_A = 16
N_HEADS_B = 16
N_KV_B = 4
GQA_GROUP = N_HEADS_B // N_KV_B
WIDTH_A = N_HEADS_A * HEAD_DIM
WIDTH_B = N_HEADS_B * HEAD_DIM
KV_WIDTH_B = N_KV_B * HEAD_DIM
PROJ_A = 3 * WIDTH_A
PROJ_B = WIDTH_B + 2 * KV_WIDTH_B
DILATIONS = (1, 4, 16)
HALF_WINDOW = 64
PARTIAL_ROT_DIM = HEAD_DIM // 4
ROPE_THETA = 500000.0
AXIAL_THETA = 10000.0
GRID_W = 64
LN_EPS = 1e-5
RMS_EPS = 1e-6
NEG_BIG = -1e30
ATTN_SCALE = HEAD_DIM ** -0.5

V7X_VMEM_BYTES = 64 * 1024 * 1024
VMEM_LIMIT_BYTES = V7X_VMEM_BYTES - 6 * 1024 * 1024
LANES = 128

F32 = jnp.float32
BF16 = jnp.bfloat16


def _params(semantics):
    return pltpu.CompilerParams(dimension_semantics=semantics, vmem_limit_bytes=VMEM_LIMIT_BYTES)


def _ada_kernel(c_ref, w_ref, b_ref, o_ref):
    c = c_ref[...]
    s = (c * (1.0 / (1.0 + jnp.exp(-c)))).astype(BF16)
    w = w_ref[...].astype(BF16)
    o_ref[...] = jnp.dot(s, w, preferred_element_type=F32) + b_ref[...]


def _ada(c_rows, w_ada, b_ada):
    depth, d, n = w_ada.shape
    rows = c_rows.shape[0]
    tn = min(512, n)
    return pl.pallas_call(
        _ada_kernel,
        out_shape=jax.ShapeDtypeStruct((depth, rows, n), F32),
        grid=(depth, n // tn),
        in_specs=[
            pl.BlockSpec((rows, d), lambda l, j: (0, 0)),
            pl.BlockSpec((None, d, tn), lambda l, j: (l, 0, j)),
            pl.BlockSpec((None, 1, tn), lambda l, j: (l, 0, j)),
        ],
        out_specs=pl.BlockSpec((None, rows, tn), lambda l, j: (l, 0, j)),
        compiler_params=_params(("parallel", "parallel")),
        name="ada_modulation",
    )(c_rows, w_ada, b_ada.reshape(depth, 1, n))


def _modulate_kernel(x_ref, sc_ref, sh_ref, h_ref):
    h_ref[...] = (x_ref[...] * (1.0 + sc_ref[...]) + sh_ref[...]).astype(BF16)


def _layernorm_rows(y, g, b):
    mu = jnp.mean(y, axis=-1, keepdims=True)
    yc = y - mu
    var = jnp.mean(yc * yc, axis=-1, keepdims=True)
    return yc * lax.rsqrt(var + LN_EPS) * g + b


def _ln_kernel(y_ref, g_ref, b_ref, x_ref):
    x_ref[...] = _layernorm_rows(y_ref[...], g_ref[...], b_ref[...])


def _ln_modulate_kernel(y_ref, g_ref, b_ref, sc_ref, sh_ref, x_ref, h_ref):
    x = _layernorm_rows(y_ref[...], g_ref[...], b_ref[...])
    x_ref[...] = x
    h_ref[...] = (x * (1.0 + sc_ref[...]) + sh_ref[...]).astype(BF16)


def _rms_rows(o, g):
    return o * lax.rsqrt(jnp.mean(o * o, axis=-1, keepdims=True) + RMS_EPS) * g


def _norm_mix_kernel(oa_ref, ob_ref, ga_ref, gb_ref, m_ref):
    wa = oa_ref.shape[-1]
    m_ref[:, :wa] = _rms_rows(oa_ref[...], ga_ref[...]).astype(BF16)
    m_ref[:, wa:] = _rms_rows(ob_ref[...], gb_ref[...]).astype(BF16)


def _row_tile(rows, seq):
    return min(256, seq, rows)


def _modulate(x, sc, sh, seq):
    rows, d = x.shape
    tm = _row_tile(rows, seq)
    per_batch = lambda i: ((i * tm) // seq, 0, 0)
    return pl.pallas_call(
        _modulate_kernel,
        out_shape=jax.ShapeDtypeStruct((rows, d), BF16),
        grid=(rows // tm,),
        in_specs=[
            pl.BlockSpec((tm, d), lambda i: (i, 0)),
            pl.BlockSpec((None, 1, d), per_batch),
            pl.BlockSpec((None, 1, d), per_batch),
        ],
        out_specs=pl.BlockSpec((tm, d), lambda i: (i, 0)),
        compiler_params=_params(("parallel",)),
        name="modulate",
    )(x, sc, sh)


def _layernorm(y, g, b, seq):
    rows, d = y.shape
    tm = _row_tile(rows, seq)
    return pl.pallas_call(
        _ln_kernel,
        out_shape=jax.ShapeDtypeStruct((rows, d), F32),
        grid=(rows // tm,),
        in_specs=[
            pl.BlockSpec((tm, d), lambda i: (i, 0)),
            pl.BlockSpec((1, d), lambda i: (0, 0)),
            pl.BlockSpec((1, d), lambda i: (0, 0)),
        ],
        out_specs=pl.BlockSpec((tm, d), lambda i: (i, 0)),
        compiler_params=_params(("parallel",)),
        name="layernorm",
    )(y, g.reshape(1, d), b.reshape(1, d))


def _layernorm_modulate(y, g, b, sc, sh, seq):
    rows, d = y.shape
    tm = _row_tile(rows, seq)
    per_batch = lambda i: ((i * tm) // seq, 0, 0)
    return pl.pallas_call(
        _ln_modulate_kernel,
        out_shape=(jax.ShapeDtypeStruct((rows, d), F32), jax.ShapeDtypeStruct((rows, d), BF16)),
        grid=(rows // tm,),
        in_specs=[
            pl.BlockSpec((tm, d), lambda i: (i, 0)),
            pl.BlockSpec((1, d), lambda i: (0, 0)),
            pl.BlockSpec((1, d), lambda i: (0, 0)),
            pl.BlockSpec((None, 1, d), per_batch),
            pl.BlockSpec((None, 1, d), per_batch),
        ],
        out_specs=(pl.BlockSpec((tm, d), lambda i: (i, 0)), pl.BlockSpec((tm, d), lambda i: (i, 0))),
        compiler_params=_params(("parallel",)),
        name="layernorm_modulate",
    )(y, g.reshape(1, d), b.reshape(1, d), sc, sh)


def _norm_mix(o_a, o_b, g_a, g_b, seq):
    rows, wa = o_a.shape
    wb = o_b.shape[1]
    tm = _row_tile(rows, seq)
    return pl.pallas_call(
        _norm_mix_kernel,
        out_shape=jax.ShapeDtypeStruct((rows, wa + wb), BF16),
        grid=(rows // tm,),
        in_specs=[
            pl.BlockSpec((tm, wa), lambda i: (i, 0)),
            pl.BlockSpec((tm, wb), lambda i: (i, 0)),
            pl.BlockSpec((1, wa), lambda i: (0, 0)),
            pl.BlockSpec((1, wb), lambda i: (0, 0)),
        ],
        out_specs=pl.BlockSpec((tm, wa + wb), lambda i: (i, 0)),
        compiler_params=_params(("parallel",)),
        name="norm_mix",
    )(o_a, o_b, g_a.reshape(1, wa), g_b.reshape(1, wb))


def _accumulate(a_ref, w_ref, acc_ref, nk, epilogue):
    part = jnp.dot(a_ref[...], w_ref[...], preferred_element_type=F32)
    if nk == 1:
        epilogue(part)
        return
    k = pl.program_id(2)

    @pl.when(k == 0)
    def _():
        acc_ref[...] = part

    @pl.when(k > 0)
    def _():
        acc_ref[...] += part

    @pl.when(k == nk - 1)
    def _():
        epilogue(acc_ref[...])


def _rotate_pairs(y, cos, sin_lo, sin_hi, half):
    return (y * cos
            + pltpu.roll(y, LANES - half, 1) * sin_lo
            + pltpu.roll(y, half, 1) * sin_hi)


def _gemm_relu2_kernel(a_ref, w_ref, o_ref, *scratch, nk):
    def epilogue(acc):
        o_ref[...] = jnp.square(jnp.maximum(acc, 0.0)).astype(o_ref.dtype)
    _accumulate(a_ref, w_ref, scratch[0] if scratch else None, nk, epilogue)


def _gemm_resid_kernel(a_ref, w_ref, x_ref, gate_ref, o_ref, *scratch, nk, alpha):
    def epilogue(acc):
        o_ref[...] = alpha * x_ref[...] + gate_ref[...] * acc
    _accumulate(a_ref, w_ref, scratch[0] if scratch else None, nk, epilogue)


def _gemm_proj_a_kernel(a_ref, w_ref, cos_ref, slo_ref, shi_ref, o_ref, *scratch, nk, rope_tiles):
    heads = o_ref.shape[0]

    def epilogue(acc):
        j = pl.program_id(1)

        @pl.when(j < rope_tiles)
        def _():
            for hh in range(heads):
                y = acc[:, hh * HEAD_DIM:(hh + 1) * HEAD_DIM]
                o_ref[hh] = _rotate_pairs(y, cos_ref[...], slo_ref[...], shi_ref[...], PARTIAL_ROT_DIM // 2)

        @pl.when(j >= rope_tiles)
        def _():
            for hh in range(heads):
                o_ref[hh] = acc[:, hh * HEAD_DIM:(hh + 1) * HEAD_DIM]

    _accumulate(a_ref, w_ref, scratch[0] if scratch else None, nk, epilogue)


def _gemm_proj_b_kernel(a_ref, w_ref, cos_ref, slo_ref, shi_ref, gq_ref, gk_ref, o_ref, *scratch, nk):
    heads = o_ref.shape[0]

    def normed_rotated(y, g):
        return _rotate_pairs(_rms_rows(y, g), cos_ref[...], slo_ref[...], shi_ref[...], HEAD_DIM // 4)

    def epilogue(acc):
        j = pl.program_id(1)
        for hh in range(heads):
            y = acc[:, hh * HEAD_DIM:(hh + 1) * HEAD_DIM]
            head = j * heads + hh

            @pl.when(head < N_HEADS_B)
            def _():
                o_ref[hh] = (normed_rotated(y, gq_ref[...]) * ATTN_SCALE).astype(o_ref.dtype)

            @pl.when(jnp.logical_and(head >= N_HEADS_B, head < N_HEADS_B + N_KV_B))
            def _():
                o_ref[hh] = normed_rotated(y, gk_ref[...]).astype(o_ref.dtype)

            @pl.when(head >= N_HEADS_B + N_KV_B)
            def _():
                o_ref[hh] = y.astype(o_ref.dtype)

    _accumulate(a_ref, w_ref, scratch[0] if scratch else None, nk, epilogue)


def _gemm_tiles(m, n, k, seq):
    tm = min(1024, seq, m)
    tn = min(1024, n)
    tk = min(2048, k)
    return tm, tn, tk


def _gemm_call(kernel_fn, a, w, extra_inputs, extra_specs, out_shape, out_spec, tiles, n_cols, col_offset, name):
    m, k = a.shape
    tm, tn, tk = tiles
    nk = k // tk
    joff = col_offset // tn
    scratch = [pltpu.VMEM((tm, tn), F32)] if nk > 1 else []
    return pl.pallas_call(
        functools.partial(kernel_fn, nk=nk),
        out_shape=out_shape,
        grid=(m // tm, n_cols // tn, nk),
        in_specs=[
            pl.BlockSpec((tm, tk), lambda i, j, kk: (i, kk)),
            pl.BlockSpec((tk, tn), lambda i, j, kk: (kk, j + joff)),
        ] + extra_specs,
        out_specs=out_spec,
        scratch_shapes=scratch,
        compiler_params=_params(("parallel", "parallel", "arbitrary")),
        name=name,
    )(a, w, *extra_inputs)


def _gemm_relu2(a, w):
    m, k = a.shape
    n = w.shape[1]
    tiles = _gemm_tiles(m, n, k, m)
    tm, tn, _ = tiles
    return _gemm_call(
        _gemm_relu2_kernel, a, w, [], [],
        jax.ShapeDtypeStruct((m, n), BF16),
        pl.BlockSpec((tm, tn), lambda i, j, kk: (i, j)),
        tiles, n, 0, "gemm_relu2")


def _gemm_resid(a, w, x, gate, seq, alpha):
    m, k = a.shape
    n = w.shape[1]
    tiles = _gemm_tiles(m, n, k, seq)
    tm, tn, _ = tiles
    return _gemm_call(
        functools.partial(_gemm_resid_kernel, alpha=alpha), a, w, [x, gate],
        [pl.BlockSpec((tm, tn), lambda i, j, kk: (i, j)),
         pl.BlockSpec((None, 1, tn), lambda i, j, kk: ((i * tm) // seq, 0, j))],
        jax.ShapeDtypeStruct((m, n), F32),
        pl.BlockSpec((tm, tn), lambda i, j, kk: (i, j)),
        tiles, n, 0, "gemm_resid")


def _rope_specs(tm, seq):
    nrow = seq // tm
    return [pl.BlockSpec((tm, LANES), lambda i, j, kk: (i % nrow, 0))] * 3


def _gemm_proj_a(h, w_in, tables, seq):
    m, k = h.shape
    tiles = _gemm_tiles(m, PROJ_A, k, seq)
    tm, tn, _ = tiles
    heads = tn // HEAD_DIM
    return _gemm_call(
        functools.partial(_gemm_proj_a_kernel, rope_tiles=2 * WIDTH_A // tn), h, w_in, list(tables),
        _rope_specs(tm, seq),
        jax.ShapeDtypeStruct((PROJ_A // HEAD_DIM, m, HEAD_DIM), F32),
        pl.BlockSpec((heads, tm, HEAD_DIM), lambda i, j, kk: (j, i, 0)),
        tiles, PROJ_A, 0, "gemm_proj_a")


def _gemm_proj_b(h, w_in, tables, g_q, g_k, seq):
    m, k = h.shape
    tiles = _gemm_tiles(m, PROJ_B, k, seq)
    tm, tn, _ = tiles
    heads = tn // HEAD_DIM
    gain_spec = pl.BlockSpec((1, HEAD_DIM), lambda i, j, kk: (0, 0))
    return _gemm_call(
        _gemm_proj_b_kernel, h, w_in,
        list(tables) + [g_q.reshape(1, HEAD_DIM), g_k.reshape(1, HEAD_DIM)],
        _rope_specs(tm, seq) + [gain_spec, gain_spec],
        jax.ShapeDtypeStruct((PROJ_B // HEAD_DIM, m, HEAD_DIM), BF16),
        pl.BlockSpec((heads, tm, HEAD_DIM), lambda i, j, kk: (j, i, 0)),
        tiles, PROJ_B, PROJ_A, "gemm_proj_b")


def _mixer_a_kernel(q_ref, k_ref, v_ref, o_ref, m_ref, d_ref, *, seq, window, plan):
    win = pl.program_id(2)

    for branch, (dil, tq) in enumerate(plan):
        length = seq // dil
        tkv = tq + 2 * HALF_WINDOW
        blocks = window // (dil * tq)

        def rows(start, size, dil=dil):
            return pl.ds(start, size) if dil == 1 else pl.ds(start, size, stride=dil)

        def unit(u, carry, dil=dil, tq=tq, tkv=tkv, blocks=blocks, length=length, rows=rows, branch=branch):
            res = lax.div(u, blocks)
            blk = lax.rem(u, blocks)
            loc = res + dil * tq * blk
            q0 = win * (window // dil) + tq * blk
            k0 = jnp.clip(q0 - HALF_WINDOW, 0, length - tkv)
            q = q_ref[rows(loc, tq), :].astype(BF16)
            k = k_ref[rows(res + dil * k0, tkv), :].astype(BF16)
            v = v_ref[rows(res + dil * k0, tkv), :].astype(BF16)
            s = lax.dot_general(q, k, (((1,), (1,)), ((), ())), preferred_element_type=F32) * ATTN_SCALE
            qi = q0 + lax.broadcasted_iota(jnp.int32, (tq, tkv), 0)
            kj = k0 + lax.broadcasted_iota(jnp.int32, (tq, tkv), 1)
            s = jnp.where(jnp.abs(kj - qi) <= HALF_WINDOW, s, NEG_BIG)
            m_blk = jnp.max(s, axis=-1, keepdims=True)
            if branch == 0:
                m_new = m_blk
                p = jnp.exp(s - m_new)
                den = jnp.sum(p, axis=-1, keepdims=True)
                acc = jnp.dot(p.astype(BF16), v, preferred_element_type=F32)
                m_rep = jnp.broadcast_to(m_new, (tq, LANES))
                d_rep = jnp.broadcast_to(den, (tq, LANES))
            else:
                m_old = m_ref[rows(loc, tq), :]
                m_rep = jnp.maximum(m_old, m_blk)
                m_new = m_rep[:, :1]
                p = jnp.exp(s - m_new)
                alpha = jnp.exp(m_old - m_rep)
                d_rep = alpha * d_ref[rows(loc, tq), :] + jnp.sum(p, axis=-1, keepdims=True)
                acc = alpha * o_ref[rows(loc, tq), :] + jnp.dot(p.astype(BF16), v, preferred_element_type=F32)
            o_ref[rows(loc, tq), :] = acc
            m_ref[rows(loc, tq), :] = m_rep
            d_ref[rows(loc, tq), :] = d_rep
            return carry

        lax.fori_loop(0, dil * blocks, unit, 0)

    o_ref[...] = o_ref[...] / d_ref[...]


def _mixer_a_plan(seq, window):
    plan = []
    for dil in DILATIONS:
        length = seq // dil
        tq = min(256, length // 2, window // dil)
        assert length >= tq + 2 * HALF_WINDOW and window % (dil * tq) == 0, (seq, window, dil)
        plan.append((dil, tq))
    return tuple(plan)


def _mixer_a(qkv, batch, seq):
    rows = batch * seq
    window = min(4096, seq)
    nwin = seq // window
    plan = _mixer_a_plan(seq, window)
    q_spec = pl.BlockSpec((None, window, HEAD_DIM), lambda b, h, w: (h, b * nwin + w, 0))
    k_spec = pl.BlockSpec((None, seq, HEAD_DIM), lambda b, h, w: (N_HEADS_A + h, b, 0))
    v_spec = pl.BlockSpec((None, seq, HEAD_DIM), lambda b, h, w: (2 * N_HEADS_A + h, b, 0))
    return pl.pallas_call(
        functools.partial(_mixer_a_kernel, seq=seq, window=window, plan=plan),
        out_shape=jax.ShapeDtypeStruct((rows, WIDTH_A), F32),
        grid=(batch, N_HEADS_A, nwin),
        in_specs=[q_spec, k_spec, v_spec],
        out_specs=pl.BlockSpec((window, HEAD_DIM), lambda b, h, w: (b * nwin + w, h)),
        scratch_shapes=[pltpu.VMEM((window, LANES), F32), pltpu.VMEM((window, LANES), F32)],
        compiler_params=_params(("parallel", "parallel", "arbitrary")),
        name="mixer_a_dilated",
    )(qkv, qkv, qkv)


def _mixer_b_kernel(q_ref, k_ref, v_ref, o_ref, m_ref, l_ref, acc_ref, *, tq, nkv):
    ki = pl.program_id(3)

    @pl.when(ki == 0)
    def _():
        m_ref[...] = jnp.full_like(m_ref, NEG_BIG)
        l_ref[...] = jnp.zeros_like(l_ref)
        acc_ref[...] = jnp.zeros_like(acc_ref)

    q = q_ref[...].reshape(GQA_GROUP * tq, HEAD_DIM)
    s = lax.dot_general(q, k_ref[...], (((1,), (1,)), ((), ())), preferred_element_type=F32)
    m_old = m_ref[...]
    m_new = jnp.maximum(m_old, jnp.max(s, axis=-1, keepdims=True))
    alpha = jnp.exp(m_old - m_new)
    p = jnp.exp(s - m_new)
    l_ref[...] = alpha * l_ref[...] + jnp.sum(p, axis=-1, keepdims=True)
    acc_ref[...] = alpha * acc_ref[...] + jnp.dot(p.astype(BF16), v_ref[...], preferred_element_type=F32)
    m_ref[...] = m_new

    @pl.when(ki == nkv - 1)
    def _():
        o = acc_ref[...] / l_ref[...]
        for hh in range(GQA_GROUP):
            o_ref[:, hh * HEAD_DIM:(hh + 1) * HEAD_DIM] = o[hh * tq:(hh + 1) * tq]


def _mixer_b(qkv, batch, seq):
    rows = batch * seq
    tq = min(256, seq)
    tkv = min(1024, seq)
    nq, nkv = seq // tq, seq // tkv
    q_spec = pl.BlockSpec((GQA_GROUP, tq, HEAD_DIM), lambda b, g, qi, ki: (g, b * nq + qi, 0))
    k_spec = pl.BlockSpec((None, tkv, HEAD_DIM), lambda b, g, qi, ki: (N_HEADS_B + g, b * nkv + ki, 0))
    v_spec = pl.BlockSpec((None, tkv, HEAD_DIM), lambda b, g, qi, ki: (N_HEADS_B + N_KV_B + g, b * nkv + ki, 0))
    return pl.pallas_call(
        functools.partial(_mixer_b_kernel, tq=tq, nkv=nkv),
        out_shape=jax.ShapeDtypeStruct((rows, WIDTH_B), F32),
        grid=(batch, N_KV_B, nq, nkv),
        in_specs=[q_spec, k_spec, v_spec],
        out_specs=pl.BlockSpec((tq, GQA_GROUP * HEAD_DIM), lambda b, g, qi, ki: (b * nq + qi, g)),
        scratch_shapes=[
            pltpu.VMEM((GQA_GROUP * tq, 1), F32),
            pltpu.VMEM((GQA_GROUP * tq, 1), F32),
            pltpu.VMEM((GQA_GROUP * tq, HEAD_DIM), F32),
        ],
        compiler_params=_params(("parallel", "parallel", "parallel", "arbitrary")),
        name="mixer_b_gqa",
    )(qkv, qkv, qkv)


def _rotary_tables(angle_groups):
    seq = angle_groups[0].shape[0]
    cos = np.ones((seq, HEAD_DIM), np.float64)
    sin_lo = np.zeros((seq, HEAD_DIM), np.float64)
    sin_hi = np.zeros((seq, HEAD_DIM), np.float64)
    lane = 0
    for ang in angle_groups:
        half = ang.shape[1]
        cos[:, lane:lane + half] = np.cos(ang)
        cos[:, lane + half:lane + 2 * half] = np.cos(ang)
        sin_lo[:, lane:lane + half] = -np.sin(ang)
        sin_hi[:, lane + half:lane + 2 * half] = np.sin(ang)
        lane += 2 * half
    return tuple(jnp.asarray(t, F32) for t in (cos, sin_lo, sin_hi))


def _inv_freq(rot_dim, theta):
    return theta ** (-np.arange(0, rot_dim, 2, dtype=np.float64) / rot_dim)


def _partial_rope_tables(seq):
    pos = np.arange(seq, dtype=np.float64)
    return _rotary_tables([pos[:, None] * _inv_freq(PARTIAL_ROT_DIM, ROPE_THETA)[None, :]])


def _axial_rope_tables(seq):
    pos = np.arange(seq)
    inv = _inv_freq(HEAD_DIM // 2, AXIAL_THETA)[None, :]
    row = (pos // GRID_W).astype(np.float64)[:, None]
    col = (pos % GRID_W).astype(np.float64)[:, None]
    return _rotary_tables([row * inv, col * inv])


def _run_group(x, mod, weights, depth):
    batch, seq, d = x.shape
    rows = batch * seq
    alpha = (2 * depth) ** 0.25
    tables_a = _partial_rope_tables(seq)
    tables_b = _axial_rope_tables(seq)
    x = x.reshape(rows, d)

    def chunks(l):
        m = mod[l].reshape(batch, 6, 1, d)
        return [m[:, i] for i in range(6)]

    sh1, sc1, g1, sh2, sc2, g2 = chunks(0)
    h = _modulate(x, sc1, sh1, seq)
    for l in range(depth):
        w = weights[l]
        qkv_a = _gemm_proj_a(h, w["w_in"], tables_a, seq)
        qkv_b = _gemm_proj_b(h, w["w_in"], tables_b, w["g_q"], w["g_k"], seq)
        o_a = _mixer_a(qkv_a, batch, seq)
        o_b = _mixer_b(qkv_b, batch, seq)
        mixed = _norm_mix(o_a, o_b, w["g_out_a"], w["g_out_b"], seq)
        y = _gemm_resid(mixed, w["w_out"], x, g1, seq, alpha)
        x, h = _layernorm_modulate(y, w["ln1_g"], w["ln1_b"], sc2, sh2, seq)
        f = _gemm_relu2(h, w["w_up"])
        y = _gemm_resid(f, w["w_down"], x, g2, seq, alpha)
        if l + 1 < depth:
            sh1, sc1, g1, sh2, sc2, g2 = chunks(l + 1)
            x, h = _layernorm_modulate(y, w["ln2_g"], w["ln2_b"], sc1, sh1, seq)
        else:
            x = _layernorm(y, w["ln2_g"], w["ln2_b"], seq)
    return x.reshape(batch, seq, d)


def kernel(x_prompt, x_sample, c_prompt, c_sample, w_ada, b_ada, w_in, g_q, g_k, g_out_a, g_out_b, w_out,
           ln1_g, ln1_b, w_up, w_down, ln2_g, ln2_b):
    depth = w_ada.shape[0]
    n_prompt, n_sample = c_prompt.shape[0], c_sample.shape[0]
    pad = -(n_prompt + n_sample) % 8
    c_rows = jnp.concatenate([c_prompt, c_sample, jnp.zeros((pad, c_prompt.shape[1]), F32)], axis=0)
    mod = _ada(c_rows, w_ada, b_ada)

    weights = [
        dict(w_in=w_in[l].astype(BF16), w_out=w_out[l].astype(BF16), w_up=w_up[l].astype(BF16),
             w_down=w_down[l].astype(BF16), g_q=g_q[l], g_k=g_k[l], g_out_a=g_out_a[l], g_out_b=g_out_b[l],
             ln1_g=ln1_g[l], ln1_b=ln1_b[l], ln2_g=ln2_g[l], ln2_b=ln2_b[l])
        for l in range(depth)
    ]
    y_prompt = _run_group(x_prompt, mod[:, :n_prompt], weights, depth)
    y_sample = _run_group(x_sample, mod[:, n_prompt:n_prompt + n_sample], weights, depth)
    return (y_prompt, y_sample)
```

```python
import functools

import numpy as np
import jax
import jax.numpy as jnp
from jax import lax
from jax.experimental import pallas as pl
from jax.experimental.pallas import tpu as pltpu

HEAD_DIM = 128
N_HEADS_A = 16
N_HEADS_B = 16
N_KV_B = 4
GQA_GROUP = N_HEADS_B // N_KV_B
WIDTH_A = N_HEADS_A * HEAD_DIM
WIDTH_B = N_HEADS_B * HEAD_DIM
KV_WIDTH_B = N_KV_B * HEAD_DIM
PROJ_A = 3 * WIDTH_A
PROJ_B = WIDTH_B + 2 * KV_WIDTH_B
DILATIONS = (1, 4, 16)
HALF_WINDOW = 64
PARTIAL_ROT_DIM = HEAD_DIM // 4
ROPE_THETA = 500000.0
AXIAL_THETA = 10000.0
GRID_W = 64
LN_EPS = 1e-5
RMS_EPS = 1e-6
NEG_BIG = -1e30
ATTN_SCALE = HEAD_DIM ** -0.5
LOG2_E = 1.4426950408889634

V7X_VMEM_BYTES = 64 * 1024 * 1024
VMEM_LIMIT_BYTES = V7X_VMEM_BYTES - 6 * 1024 * 1024
LANES = 128

F32 = jnp.float32
BF16 = jnp.bfloat16


def _params(semantics):
    return pltpu.CompilerParams(dimension_semantics=semantics, vmem_limit_bytes=VMEM_LIMIT_BYTES)


def _ada_kernel(c_ref, w_ref, b_ref, o_ref):
    c = c_ref[...]
    s = (c * (1.0 / (1.0 + jnp.exp(-c)))).astype(BF16)
    w = w_ref[...].astype(BF16)
    o_ref[...] = jnp.dot(s, w, preferred_element_type=F32) + b_ref[...]


def _ada(c_rows, w_ada, b_ada):
    depth, d, n = w_ada.shape
    rows = c_rows.shape[0]
    tn = min(512, n)
    return pl.pallas_call(
        _ada_kernel,
        out_shape=jax.ShapeDtypeStruct((depth, rows, n), F32),
        grid=(depth, n // tn),
        in_specs=[
            pl.BlockSpec((rows, d), lambda l, j: (0, 0)),
            pl.BlockSpec((None, d, tn), lambda l, j: (l, 0, j)),
            pl.BlockSpec((None, 1, tn), lambda l, j: (l, 0, j)),
        ],
        out_specs=pl.BlockSpec((None, rows, tn), lambda l, j: (l, 0, j)),
        compiler_params=_params(("parallel", "parallel")),
        name="ada_modulation",
    )(c_rows, w_ada, b_ada.reshape(depth, 1, n))


def _modulate_kernel(x_ref, sc_ref, sh_ref, h_ref):
    h_ref[...] = (x_ref[...] * (1.0 + sc_ref[...]) + sh_ref[...]).astype(BF16)


def _layernorm_rows(y, g, b):
    mu = jnp.mean(y, axis=-1, keepdims=True)
    yc = y - mu
    var = jnp.mean(yc * yc, axis=-1, keepdims=True)
    return yc * lax.rsqrt(var + LN_EPS) * g + b


def _ln_kernel(y_ref, g_ref, b_ref, x_ref):
    x_ref[...] = _layernorm_rows(y_ref[...], g_ref[...], b_ref[...])


def _ln_modulate_kernel(y_ref, g_ref, b_ref, sc_ref, sh_ref, x_ref, h_ref):
    x = _layernorm_rows(y_ref[...], g_ref[...], b_ref[...])
    x_ref[...] = x
    h_ref[...] = (x * (1.0 + sc_ref[...]) + sh_ref[...]).astype(BF16)


def _rms_rows(o, g):
    return o * lax.rsqrt(jnp.mean(o * o, axis=-1, keepdims=True) + RMS_EPS) * g


def _norm_mix_kernel(oa_ref, ob_ref, ga_ref, gb_ref, m_ref):
    wa = oa_ref.shape[-1]
    m_ref[:, :wa] = _rms_rows(oa_ref[...], ga_ref[...]).astype(BF16)
    m_ref[:, wa:] = _rms_rows(ob_ref[...], gb_ref[...]).astype(BF16)


def _row_tile(rows, seq):
    return min(256, seq, rows)


def _modulate(x, sc, sh, seq):
    rows, d = x.shape
    tm = _row_tile(rows, seq)
    per_batch = lambda i: ((i * tm) // seq, 0, 0)
    return pl.pallas_call(
        _modulate_kernel,
        out_shape=jax.ShapeDtypeStruct((rows, d), BF16),
        grid=(rows // tm,),
        in_specs=[
            pl.BlockSpec((tm, d), lambda i: (i, 0)),
            pl.BlockSpec((None, 1, d), per_batch),
            pl.BlockSpec((None, 1, d), per_batch),
        ],
        out_specs=pl.BlockSpec((tm, d), lambda i: (i, 0)),
        compiler_params=_params(("parallel",)),
        name="modulate",
    )(x, sc, sh)


def _layernorm(y, g, b, seq):
    rows, d = y.shape
    tm = _row_tile(rows, seq)
    return pl.pallas_call(
        _ln_kernel,
        out_shape=jax.ShapeDtypeStruct((rows, d), F32),
        grid=(rows // tm,),
        in_specs=[
            pl.BlockSpec((tm, d), lambda i: (i, 0)),
            pl.BlockSpec((1, d), lambda i: (0, 0)),
            pl.BlockSpec((1, d), lambda i: (0, 0)),
        ],
        out_specs=pl.BlockSpec((tm, d), lambda i: (i, 0)),
        compiler_params=_params(("parallel",)),
        name="layernorm",
    )(y, g.reshape(1, d), b.reshape(1, d))


def _layernorm_modulate(y, g, b, sc, sh, seq):
    rows, d = y.shape
    tm = _row_tile(rows, seq)
    per_batch = lambda i: ((i * tm) // seq, 0, 0)
    return pl.pallas_call(
        _ln_modulate_kernel,
        out_shape=(jax.ShapeDtypeStruct((rows, d), F32), jax.ShapeDtypeStruct((rows, d), BF16)),
        grid=(rows // tm,),
        in_specs=[
            pl.BlockSpec((tm, d), lambda i: (i, 0)),
            pl.BlockSpec((1, d), lambda i: (0, 0)),
            pl.BlockSpec((1, d), lambda i: (0, 0)),
            pl.BlockSpec((None, 1, d), per_batch),
            pl.BlockSpec((None, 1, d), per_batch),
        ],
        out_specs=(pl.BlockSpec((tm, d), lambda i: (i, 0)), pl.BlockSpec((tm, d), lambda i: (i, 0))),
        compiler_params=_params(("parallel",)),
        name="layernorm_modulate",
    )(y, g.reshape(1, d), b.reshape(1, d), sc, sh)


def _norm_mix(o_a, o_b, g_a, g_b, seq):
    rows, wa = o_a.shape
    wb = o_b.shape[1]
    tm = _row_tile(rows, seq)
    return pl.pallas_call(
        _norm_mix_kernel,
        out_shape=jax.ShapeDtypeStruct((rows, wa + wb), BF16),
        grid=(rows // tm,),
        in_specs=[
            pl.BlockSpec((tm, wa), lambda i: (i, 0)),
            pl.BlockSpec((tm, wb), lambda i: (i, 0)),
            pl.BlockSpec((1, wa), lambda i: (0, 0)),
            pl.BlockSpec((1, wb), lambda i: (0, 0)),
        ],
        out_specs=pl.BlockSpec((tm, wa + wb), lambda i: (i, 0)),
        compiler_params=_params(("parallel",)),
        name="norm_mix",
    )(o_a, o_b, g_a.reshape(1, wa), g_b.reshape(1, wb))


def _accumulate(a_ref, w_ref, acc_ref, nk, epilogue):
    part = jnp.dot(a_ref[...], w_ref[...], preferred_element_type=F32)
    if nk == 1:
        epilogue(part)
        return
    k = pl.program_id(2)

    @pl.when(k == 0)
    def _():
        acc_ref[...] = part

    @pl.when(k > 0)
    def _():
        acc_ref[...] += part

    @pl.when(k == nk - 1)
    def _():
        epilogue(acc_ref[...])


def _rotate_pairs(y, cos, sin_lo, sin_hi, half):
    return (y * cos
            + pltpu.roll(y, LANES - half, 1) * sin_lo
            + pltpu.roll(y, half, 1) * sin_hi)


def _gemm_relu2_kernel(a_ref, w_ref, o_ref, *scratch, nk):
    def epilogue(acc):
        o_ref[...] = jnp.square(jnp.maximum(acc, 0.0)).astype(o_ref.dtype)
    _accumulate(a_ref, w_ref, scratch[0] if scratch else None, nk, epilogue)


def _gemm_resid_kernel(a_ref, w_ref, x_ref, gate_ref, o_ref, *scratch, nk, alpha):
    def epilogue(acc):
        o_ref[...] = alpha * x_ref[...] + gate_ref[...] * acc
    _accumulate(a_ref, w_ref, scratch[0] if scratch else None, nk, epilogue)


def _gemm_proj_a_kernel(a_ref, w_ref, cos_ref, slo_ref, shi_ref, o_ref, *scratch, nk, rope_tiles):
    heads = o_ref.shape[0]

    def epilogue(acc):
        j = pl.program_id(1)

        @pl.when(j < rope_tiles)
        def _():
            for hh in range(heads):
                y = acc[:, hh * HEAD_DIM:(hh + 1) * HEAD_DIM]
                o_ref[hh] = _rotate_pairs(y, cos_ref[...], slo_ref[...], shi_ref[...], PARTIAL_ROT_DIM // 2)

        @pl.when(j >= rope_tiles)
        def _():
            for hh in range(heads):
                o_ref[hh] = acc[:, hh * HEAD_DIM:(hh + 1) * HEAD_DIM]

    _accumulate(a_ref, w_ref, scratch[0] if scratch else None, nk, epilogue)


def _axial_normed_rotated(y, g_ref, cos_ref, slo_ref, shi_ref):
    return _rotate_pairs(_rms_rows(y, g_ref[...]), cos_ref[...], slo_ref[...], shi_ref[...], HEAD_DIM // 4)


def _gemm_proj_bq_kernel(a_ref, w_ref, cos_ref, slo_ref, shi_ref, gq_ref, o_ref, *scratch, nk):
    def epilogue(acc):
        for hh in range(o_ref.shape[0]):
            y = acc[:, hh * HEAD_DIM:(hh + 1) * HEAD_DIM]
            q = _axial_normed_rotated(y, gq_ref, cos_ref, slo_ref, shi_ref)
            o_ref[hh] = (q * (ATTN_SCALE * LOG2_E)).astype(o_ref.dtype)

    _accumulate(a_ref, w_ref, scratch[0] if scratch else None, nk, epilogue)


def _gemm_proj_bkv_kernel(a_ref, w_ref, cos_ref, slo_ref, shi_ref, gk_ref, k_ref, vt_ref, *scratch, nk):
    def epilogue(acc):
        for hh in range(N_KV_B):
            y = acc[:, hh * HEAD_DIM:(hh + 1) * HEAD_DIM]
            k_ref[hh] = _axial_normed_rotated(y, gk_ref, cos_ref, slo_ref, shi_ref).astype(k_ref.dtype)
        for hh in range(N_KV_B):
            y = acc[:, (N_KV_B + hh) * HEAD_DIM:(N_KV_B + hh + 1) * HEAD_DIM]
            vt_ref[hh] = y.T.astype(vt_ref.dtype)

    _accumulate(a_ref, w_ref, scratch[0] if scratch else None, nk, epilogue)


def _gemm_tiles(m, n, k, seq):
    tm = min(1024, seq, m)
    tn = min(1024, n)
    tk = k if k <= 4096 else 2048
    return tm, tn, tk


def _gemm_call(kernel_fn, a, w, extra_inputs, extra_specs, out_shape, out_spec, tiles, n_cols, col_offset, name):
    m, k = a.shape
    tm, tn, tk = tiles
    nk = k // tk
    joff = col_offset // tn
    scratch = [pltpu.VMEM((tm, tn), F32)] if nk > 1 else []
    return pl.pallas_call(
        functools.partial(kernel_fn, nk=nk),
        out_shape=out_shape,
        grid=(m // tm, n_cols // tn, nk),
        in_specs=[
            pl.BlockSpec((tm, tk), lambda i, j, kk: (i, kk)),
            pl.BlockSpec((tk, tn), lambda i, j, kk: (kk, j + joff)),
        ] + extra_specs,
        out_specs=out_spec,
        scratch_shapes=scratch,
        compiler_params=_params(("parallel", "parallel", "arbitrary")),
        name=name,
    )(a, w, *extra_inputs)


def _gemm_relu2(a, w):
    m, k = a.shape
    n = w.shape[1]
    tiles = _gemm_tiles(m, n, k, m)
    tm, tn, _ = tiles
    return _gemm_call(
        _gemm_relu2_kernel, a, w, [], [],
        jax.ShapeDtypeStruct((m, n), BF16),
        pl.BlockSpec((tm, tn), lambda i, j, kk: (i, j)),
        tiles, n, 0, "gemm_relu2")


def _gemm_resid(a, w, x, gate, seq, alpha):
    m, k = a.shape
    n = w.shape[1]
    tiles = _gemm_tiles(m, n, k, seq)
    tm, tn, _ = tiles
    return _gemm_call(
        functools.partial(_gemm_resid_kernel, alpha=alpha), a, w, [x, gate],
        [pl.BlockSpec((tm, tn), lambda i, j, kk: (i, j)),
         pl.BlockSpec((None, 1, tn), lambda i, j, kk: ((i * tm) // seq, 0, j))],
        jax.ShapeDtypeStruct((m, n), F32),
        pl.BlockSpec((tm, tn), lambda i, j, kk: (i, j)),
        tiles, n, 0, "gemm_resid")


def _rope_specs(tm, seq):
    nrow = seq // tm
    return [pl.BlockSpec((tm, LANES), lambda i, j, kk: (i % nrow, 0))] * 3


def _gemm_proj_a(h, w_in, tables, seq):
    m, k = h.shape
    tiles = _gemm_tiles(m, PROJ_A, k, seq)
    tm, tn, _ = tiles
    heads = tn // HEAD_DIM
    return _gemm_call(
        functools.partial(_gemm_proj_a_kernel, rope_tiles=2 * WIDTH_A // tn), h, w_in, list(tables),
        _rope_specs(tm, seq),
        jax.ShapeDtypeStruct((PROJ_A // HEAD_DIM, m, HEAD_DIM), F32),
        pl.BlockSpec((heads, tm, HEAD_DIM), lambda i, j, kk: (j, i, 0)),
        tiles, PROJ_A, 0, "gemm_proj_a")


def _gemm_proj_bq(h, w_in, tables, g_q, seq):
    m, k = h.shape
    tiles = _gemm_tiles(m, WIDTH_B, k, seq)
    tm, tn, _ = tiles
    heads = tn // HEAD_DIM
    return _gemm_call(
        _gemm_proj_bq_kernel, h, w_in, list(tables) + [g_q.reshape(1, HEAD_DIM)],
        _rope_specs(tm, seq) + [pl.BlockSpec((1, HEAD_DIM), lambda i, j, kk: (0, 0))],
        jax.ShapeDtypeStruct((N_HEADS_B, m, HEAD_DIM), BF16),
        pl.BlockSpec((heads, tm, HEAD_DIM), lambda i, j, kk: (j, i, 0)),
        tiles, WIDTH_B, PROJ_A, "gemm_proj_bq")


def _gemm_proj_bkv(h, w_in, tables, g_k, seq):
    m, k = h.shape
    tm, _, tk = _gemm_tiles(m, 2 * KV_WIDTH_B, k, seq)
    tiles = (tm, 2 * KV_WIDTH_B, tk)
    return _gemm_call(
        _gemm_proj_bkv_kernel, h, w_in, list(tables) + [g_k.reshape(1, HEAD_DIM)],
        _rope_specs(tm, seq) + [pl.BlockSpec((1, HEAD_DIM), lambda i, j, kk: (0, 0))],
        (jax.ShapeDtypeStruct((N_KV_B, m, HEAD_DIM), BF16), jax.ShapeDtypeStruct((N_KV_B, HEAD_DIM, m), BF16)),
        (pl.BlockSpec((N_KV_B, tm, HEAD_DIM), lambda i, j, kk: (0, i, 0)),
         pl.BlockSpec((N_KV_B, HEAD_DIM, tm), lambda i, j, kk: (0, 0, i))),
        tiles, 2 * KV_WIDTH_B, PROJ_A + WIDTH_B, "gemm_proj_bkv")


def _mixer_a_kernel(q_ref, k_ref, v_ref, o_ref, acc_ref, m_ref, d_ref, *, seq, window, plan):
    win = pl.program_id(2)

    for branch, (dil, tq) in enumerate(plan):
        length = seq // dil
        tkv = tq + 2 * HALF_WINDOW
        blocks = window // (dil * tq)
        offset = (lax.broadcasted_iota(jnp.int32, (tq, tkv), 1)
                  - lax.broadcasted_iota(jnp.int32, (tq, tkv), 0))

        def rows(start, size, dil=dil):
            return pl.ds(start, size) if dil == 1 else pl.ds(start, size, stride=dil)

        for res in range(dil):
            for blk in range(blocks):
                loc = res + dil * tq * blk
                q0 = win * (window // dil) + tq * blk
                k0 = jnp.clip(q0 - HALF_WINDOW, 0, length - tkv)
                q = q_ref[rows(loc, tq), :].astype(BF16)
                k = k_ref[rows(res + dil * k0, tkv), :].astype(BF16)
                v = v_ref[rows(res + dil * k0, tkv), :].astype(BF16)
                s = lax.dot_general(q, k, (((1,), (1,)), ((), ())), preferred_element_type=F32) * ATTN_SCALE
                shifted = offset + (k0 - q0 + HALF_WINDOW)
                valid = lax.bitcast_convert_type(shifted, jnp.uint32) <= 2 * HALF_WINDOW
                s = jnp.where(valid, s, NEG_BIG)
                m = jnp.max(s, axis=-1, keepdims=True)
                p = jnp.exp(s - m)
                den = jnp.sum(p, axis=-1, keepdims=True)
                acc_ref[branch, rows(loc, tq), :] = jnp.dot(p.astype(BF16), v, preferred_element_type=F32)
                m_ref[branch, rows(loc, tq), :] = jnp.broadcast_to(m, (tq, LANES))
                d_ref[branch, rows(loc, tq), :] = jnp.broadcast_to(den, (tq, LANES))

    m_all = m_ref[0]
    for branch in range(1, len(plan)):
        m_all = jnp.maximum(m_all, m_ref[branch])
    num = jnp.zeros(o_ref.shape, F32)
    tot = jnp.zeros(o_ref.shape, F32)
    for branch in range(len(plan)):
        w = jnp.exp(m_ref[branch] - m_all)
        num = num + w * acc_ref[branch]
        tot = tot + w * d_ref[branch]
    o_ref[...] = num / tot


def _mixer_a_plan(seq, window):
    plan = []
    for dil in DILATIONS:
        length = seq // dil
        tq = min(128, length // 2, window // dil)
        assert length >= tq + 2 * HALF_WINDOW and window % (dil * tq) == 0, (seq, window, dil)
        plan.append((dil, tq))
    return tuple(plan)


def _mixer_a(qkv, batch, seq):
    rows = batch * seq
    window = min(2048, seq)
    nwin = seq // window
    plan = _mixer_a_plan(seq, window)
    q_spec = pl.BlockSpec((None, window, HEAD_DIM), lambda b, h, w: (h, b * nwin + w, 0))
    k_spec = pl.BlockSpec((None, seq, HEAD_DIM), lambda b, h, w: (N_HEADS_A + h, b, 0))
    v_spec = pl.BlockSpec((None, seq, HEAD_DIM), lambda b, h, w: (2 * N_HEADS_A + h, b, 0))
    stat = pltpu.VMEM((len(plan), window, LANES), F32)
    return pl.pallas_call(
        functools.partial(_mixer_a_kernel, seq=seq, window=window, plan=plan),
        out_shape=jax.ShapeDtypeStruct((rows, WIDTH_A), F32),
        grid=(batch, N_HEADS_A, nwin),
        in_specs=[q_spec, k_spec, v_spec],
        out_specs=pl.BlockSpec((window, HEAD_DIM), lambda b, h, w: (b * nwin + w, h)),
        scratch_shapes=[stat, stat, stat],
        compiler_params=_params(("parallel", "parallel", "arbitrary")),
        name="mixer_a_dilated",
    )(qkv, qkv, qkv)


SUBLANES = 8
MIXER_B_CHAINS = 4


def _reduce_rows(x, op):
    parts = [x[r:r + SUBLANES] for r in range(0, x.shape[0], SUBLANES)]
    while len(parts) > 1:
        parts = [op(parts[i], parts[i + 1]) for i in range(0, len(parts) - 1, 2)] + (
            [parts[-1]] if len(parts) % 2 else [])
    top = parts[0]
    rows = [top[r:r + 1] for r in range(top.shape[0])]
    while len(rows) > 1:
        rows = [op(rows[i], rows[i + 1]) for i in range(0, len(rows), 2)]
    return rows[0]


def _mixer_b_kernel(q_ref, k_ref, vt_ref, o_ref, m_ref, l_ref, acc_ref, *, nkv, chains):
    ki = pl.program_id(3)

    @pl.when(ki == 0)
    def _():
        m_ref[...] = jnp.full_like(m_ref, NEG_BIG)
        l_ref[...] = jnp.zeros_like(l_ref)
        acc_ref[...] = jnp.zeros_like(acc_ref)

    k = k_ref[...]
    vt = vt_ref[...]
    heads_per_chain = GQA_GROUP // chains
    tq = q_ref.shape[1]
    scores, maxima = [], []
    for c in range(chains):
        q = q_ref[c * heads_per_chain:(c + 1) * heads_per_chain].reshape(heads_per_chain * tq, HEAD_DIM)
        st = lax.dot_general(k, q, (((1,), (1,)), ((), ())), preferred_element_type=F32)
        scores.append(st)
        maxima.append(jnp.maximum(m_ref[c], _reduce_rows(st, jnp.maximum)))
    for c in range(chains):
        m_old, m_new = m_ref[c], maxima[c]
        alpha = jnp.exp2(m_old - m_new)
        p = jnp.exp2(scores[c] - m_new)
        l_ref[c] = alpha * l_ref[c] + _reduce_rows(p, jnp.add)
        acc_ref[c] = alpha * acc_ref[c] + jnp.dot(vt, p.astype(BF16), preferred_element_type=F32)
        m_ref[c] = m_new

    @pl.when(ki == nkv - 1)
    def _():
        for c in range(chains):
            o = (acc_ref[c] / l_ref[c]).T
            for hh in range(heads_per_chain):
                head = c * heads_per_chain + hh
                o_ref[:, head * HEAD_DIM:(head + 1) * HEAD_DIM] = o[hh * tq:(hh + 1) * tq]


def _mixer_b(q, k, vt, batch, seq):
    rows = batch * seq
    tq = min(256, seq)
    tkv = min(1024, seq)
    nq, nkv = seq // tq, seq // tkv
    cols = GQA_GROUP // MIXER_B_CHAINS * tq
    q_spec = pl.BlockSpec((GQA_GROUP, tq, HEAD_DIM), lambda b, g, qi, ki: (g, b * nq + qi, 0))
    k_spec = pl.BlockSpec((None, tkv, HEAD_DIM), lambda b, g, qi, ki: (g, b * nkv + ki, 0))
    vt_spec = pl.BlockSpec((None, HEAD_DIM, tkv), lambda b, g, qi, ki: (g, 0, b * nkv + ki))
    return pl.pallas_call(
        functools.partial(_mixer_b_kernel, nkv=nkv, chains=MIXER_B_CHAINS),
        out_shape=jax.ShapeDtypeStruct((rows, WIDTH_B), F32),
        grid=(batch, N_KV_B, nq, nkv),
        in_specs=[q_spec, k_spec, vt_spec],
        out_specs=pl.BlockSpec((tq, GQA_GROUP * HEAD_DIM), lambda b, g, qi, ki: (b * nq + qi, g)),
        scratch_shapes=[
            pltpu.VMEM((MIXER_B_CHAINS, 1, cols), F32),
            pltpu.VMEM((MIXER_B_CHAINS, 1, cols), F32),
            pltpu.VMEM((MIXER_B_CHAINS, HEAD_DIM, cols), F32),
        ],
        compiler_params=_params(("parallel", "parallel", "parallel", "arbitrary")),
        name="mixer_b_gqa",
    )(q, k, vt)


def _rotary_tables(angle_groups):
    seq = angle_groups[0].shape[0]
    cos = np.ones((seq, HEAD_DIM), np.float64)
    sin_lo = np.zeros((seq, HEAD_DIM), np.float64)
    sin_hi = np.zeros((seq, HEAD_DIM), np.float64)
    lane = 0
    for ang in angle_groups:
        half = ang.shape[1]
        cos[:, lane:lane + half] = np.cos(ang)
        cos[:, lane + half:lane + 2 * half] = np.cos(ang)
        sin_lo[:, lane:lane + half] = -np.sin(ang)
        sin_hi[:, lane + half:lane + 2 * half] = np.sin(ang)
        lane += 2 * half
    return tuple(jnp.asarray(t, F32) for t in (cos, sin_lo, sin_hi))


def _inv_freq(rot_dim, theta):
    return theta ** (-np.arange(0, rot_dim, 2, dtype=np.float64) / rot_dim)


def _partial_rope_tables(seq):
    pos = np.arange(seq, dtype=np.float64)
    return _rotary_tables([pos[:, None] * _inv_freq(PARTIAL_ROT_DIM, ROPE_THETA)[None, :]])


def _axial_rope_tables(seq):
    pos = np.arange(seq)
    inv = _inv_freq(HEAD_DIM // 2, AXIAL_THETA)[None, :]
    row = (pos // GRID_W).astype(np.float64)[:, None]
    col = (pos % GRID_W).astype(np.float64)[:, None]
    return _rotary_tables([row * inv, col * inv])


def _run_group(x, mod, weights, depth):
    batch, seq, d = x.shape
    rows = batch * seq
    alpha = (2 * depth) ** 0.25
    tables_a = _partial_rope_tables(seq)
    tables_b = _axial_rope_tables(seq)
    x = x.reshape(rows, d)

    def chunks(l):
        m = mod[l].reshape(batch, 6, 1, d)
        return [m[:, i] for i in range(6)]

    sh1, sc1, g1, sh2, sc2, g2 = chunks(0)
    h = _modulate(x, sc1, sh1, seq)
    for l in range(depth):
        w = weights[l]
        qkv_a = _gemm_proj_a(h, w["w_in"], tables_a, seq)
        q_b = _gemm_proj_bq(h, w["w_in"], tables_b, w["g_q"], seq)
        k_b, vt_b = _gemm_proj_bkv(h, w["w_in"], tables_b, w["g_k"], seq)
        o_a = _mixer_a(qkv_a, batch, seq)
        o_b = _mixer_b(q_b, k_b, vt_b, batch, seq)
        mixed = _norm_mix(o_a, o_b, w["g_out_a"], w["g_out_b"], seq)
        y = _gemm_resid(mixed, w["w_out"], x, g1, seq, alpha)
        x, h = _layernorm_modulate(y, w["ln1_g"], w["ln1_b"], sc2, sh2, seq)
        f = _gemm_relu2(h, w["w_up"])
        y = _gemm_resid(f, w["w_down"], x, g2, seq, alpha)
        if l + 1 < depth:
            sh1, sc1, g1, sh2, sc2, g2 = chunks(l + 1)
            x, h = _layernorm_modulate(y, w["ln2_g"], w["ln2_b"], sc1, sh1, seq)
        else:
            x = _layernorm(y, w["ln2_g"], w["ln2_b"], seq)
    return x.reshape(batch, seq, d)


def kernel(x_prompt, x_sample, c_prompt, c_sample, w_ada, b_ada, w_in, g_q, g_k, g_out_a, g_out_b, w_out,
           ln1_g, ln1_b, w_up, w_down, ln2_g, ln2_b):
    depth = w_ada.shape[0]
    n_prompt, n_sample = c_prompt.shape[0], c_sample.shape[0]
    pad = -(n_prompt + n_sample) % 8
    c_rows = jnp.concatenate([c_prompt, c_sample, jnp.zeros((pad, c_prompt.shape[1]), F32)], axis=0)
    mod = _ada(c_rows, w_ada, b_ada)

    weights = [
        dict(w_in=w_in[l].astype(BF16), w_out=w_out[l].astype(BF16), w_up=w_up[l].astype(BF16),
             w_down=w_down[l].astype(BF16), g_q=g_q[l], g_k=g_k[l], g_out_a=g_out_a[l], g_out_b=g_out_b[l],
             ln1_g=ln1_g[l], ln1_b=ln1_b[l], ln2_g=ln2_g[l], ln2_b=ln2_b[l])
        for l in range(depth)
    ]
    y_prompt = _run_group(x_prompt, mod[:, :n_prompt], weights, depth)
    y_sample = _run_group(x_sample, mod[:, n_prompt:n_prompt + n_sample], weights, depth)
    return (y_prompt, y_sample)
```

```python
import functools

import numpy as np
import jax
import jax.numpy as jnp
from jax import lax
from jax.experimental import pallas as pl
from jax.experimental.pallas import tpu as pltpu

HEAD_DIM = 128
N_HEADS_A = 16
N_HEADS_B = 16
N_KV_B = 4
GQA_GROUP = N_HEADS_B // N_KV_B
WIDTH_A = N_HEADS_A * HEAD_DIM
WIDTH_B = N_HEADS_B * HEAD_DIM
KV_WIDTH_B = N_KV_B * HEAD_DIM
PROJ_A = 3 * WIDTH_A
PROJ_B = WIDTH_B + 2 * KV_WIDTH_B
DILATIONS = (1, 4, 16)
HALF_WINDOW = 64
PARTIAL_ROT_DIM = HEAD_DIM // 4
ROPE_THETA = 500000.0
AXIAL_THETA = 10000.0
GRID_W = 64
LN_EPS = 1e-5
RMS_EPS = 1e-6
NEG_BIG = -1e30
ATTN_SCALE = HEAD_DIM ** -0.5
LOG2_E = 1.4426950408889634

V7X_VMEM_BYTES = 64 * 1024 * 1024
VMEM_LIMIT_BYTES = V7X_VMEM_BYTES - 6 * 1024 * 1024
LANES = 128

F32 = jnp.float32
BF16 = jnp.bfloat16


def _params(semantics):
    return pltpu.CompilerParams(dimension_semantics=semantics, vmem_limit_bytes=VMEM_LIMIT_BYTES)


def _ada_kernel(c_ref, w_ref, b_ref, o_ref):
    c = c_ref[...]
    s = (c * (1.0 / (1.0 + jnp.exp(-c)))).astype(BF16)
    w = w_ref[...].astype(BF16)
    o_ref[...] = jnp.dot(s, w, preferred_element_type=F32) + b_ref[...]


def _ada(c_rows, w_ada, b_ada):
    depth, d, n = w_ada.shape
    rows = c_rows.shape[0]
    tn = min(512, n)
    return pl.pallas_call(
        _ada_kernel,
        out_shape=jax.ShapeDtypeStruct((depth, rows, n), F32),
        grid=(depth, n // tn),
        in_specs=[
            pl.BlockSpec((rows, d), lambda l, j: (0, 0)),
            pl.BlockSpec((None, d, tn), lambda l, j: (l, 0, j)),
            pl.BlockSpec((None, 1, tn), lambda l, j: (l, 0, j)),
        ],
        out_specs=pl.BlockSpec((None, rows, tn), lambda l, j: (l, 0, j)),
        compiler_params=_params(("parallel", "parallel")),
        name="ada_modulation",
    )(c_rows, w_ada, b_ada.reshape(depth, 1, n))


def _modulate_kernel(x_ref, sc_ref, sh_ref, h_ref):
    h_ref[...] = (x_ref[...] * (1.0 + sc_ref[...]) + sh_ref[...]).astype(BF16)


def _layernorm_rows(y, g, b):
    mu = jnp.mean(y, axis=-1, keepdims=True)
    yc = y - mu
    var = jnp.mean(yc * yc, axis=-1, keepdims=True)
    return yc * lax.rsqrt(var + LN_EPS) * g + b


def _ln_kernel(y_ref, g_ref, b_ref, x_ref):
    x_ref[...] = _layernorm_rows(y_ref[...], g_ref[...], b_ref[...])


def _ln_modulate_kernel(y_ref, g_ref, b_ref, sc_ref, sh_ref, x_ref, h_ref):
    x = _layernorm_rows(y_ref[...], g_ref[...], b_ref[...])
    x_ref[...] = x
    h_ref[...] = (x * (1.0 + sc_ref[...]) + sh_ref[...]).astype(BF16)


def _rms_rows(o, g):
    return o * lax.rsqrt(jnp.mean(o * o, axis=-1, keepdims=True) + RMS_EPS) * g


def _norm_mix_kernel(oa_ref, ob_ref, ga_ref, gb_ref, m_ref):
    wa = oa_ref.shape[-1]
    m_ref[:, :wa] = _rms_rows(oa_ref[...], ga_ref[...]).astype(BF16)
    m_ref[:, wa:] = _rms_rows(ob_ref[...], gb_ref[...]).astype(BF16)


def _row_tile(rows, seq):
    return min(256, seq, rows)


def _modulate(x, sc, sh, seq):
    rows, d = x.shape
    tm = _row_tile(rows, seq)
    per_batch = lambda i: ((i * tm) // seq, 0, 0)
    return pl.pallas_call(
        _modulate_kernel,
        out_shape=jax.ShapeDtypeStruct((rows, d), BF16),
        grid=(rows // tm,),
        in_specs=[
            pl.BlockSpec((tm, d), lambda i: (i, 0)),
            pl.BlockSpec((None, 1, d), per_batch),
            pl.BlockSpec((None, 1, d), per_batch),
        ],
        out_specs=pl.BlockSpec((tm, d), lambda i: (i, 0)),
        compiler_params=_params(("parallel",)),
        name="modulate",
    )(x, sc, sh)


def _layernorm(y, g, b, seq):
    rows, d = y.shape
    tm = _row_tile(rows, seq)
    return pl.pallas_call(
        _ln_kernel,
        out_shape=jax.ShapeDtypeStruct((rows, d), F32),
        grid=(rows // tm,),
        in_specs=[
            pl.BlockSpec((tm, d), lambda i: (i, 0)),
            pl.BlockSpec((1, d), lambda i: (0, 0)),
            pl.BlockSpec((1, d), lambda i: (0, 0)),
        ],
        out_specs=pl.BlockSpec((tm, d), lambda i: (i, 0)),
        compiler_params=_params(("parallel",)),
        name="layernorm",
    )(y, g.reshape(1, d), b.reshape(1, d))


def _layernorm_modulate(y, g, b, sc, sh, seq):
    rows, d = y.shape
    tm = _row_tile(rows, seq)
    per_batch = lambda i: ((i * tm) // seq, 0, 0)
    return pl.pallas_call(
        _ln_modulate_kernel,
        out_shape=(jax.ShapeDtypeStruct((rows, d), F32), jax.ShapeDtypeStruct((rows, d), BF16)),
        grid=(rows // tm,),
        in_specs=[
            pl.BlockSpec((tm, d), lambda i: (i, 0)),
            pl.BlockSpec((1, d), lambda i: (0, 0)),
            pl.BlockSpec((1, d), lambda i: (0, 0)),
            pl.BlockSpec((None, 1, d), per_batch),
            pl.BlockSpec((None, 1, d), per_batch),
        ],
        out_specs=(pl.BlockSpec((tm, d), lambda i: (i, 0)), pl.BlockSpec((tm, d), lambda i: (i, 0))),
        compiler_params=_params(("parallel",)),
        name="layernorm_modulate",
    )(y, g.reshape(1, d), b.reshape(1, d), sc, sh)


def _norm_mix(o_a, o_b, g_a, g_b, seq):
    rows, wa = o_a.shape
    wb = o_b.shape[1]
    tm = _row_tile(rows, seq)
    return pl.pallas_call(
        _norm_mix_kernel,
        out_shape=jax.ShapeDtypeStruct((rows, wa + wb), BF16),
        grid=(rows // tm,),
        in_specs=[
            pl.BlockSpec((tm, wa), lambda i: (i, 0)),
            pl.BlockSpec((tm, wb), lambda i: (i, 0)),
            pl.BlockSpec((1, wa), lambda i: (0, 0)),
            pl.BlockSpec((1, wb), lambda i: (0, 0)),
        ],
        out_specs=pl.BlockSpec((tm, wa + wb), lambda i: (i, 0)),
        compiler_params=_params(("parallel",)),
        name="norm_mix",
    )(o_a, o_b, g_a.reshape(1, wa), g_b.reshape(1, wb))


def _accumulate(a_ref, w_ref, acc_ref, nk, epilogue):
    def part():
        return jnp.dot(a_ref[...], w_ref[...], preferred_element_type=F32)

    if nk == 1:
        epilogue(part())
        return
    k = pl.program_id(2)

    @pl.when(k == 0)
    def _():
        acc_ref[...] = part()

    @pl.when(jnp.logical_and(k > 0, k < nk - 1))
    def _():
        acc_ref[...] += part()

    @pl.when(k == nk - 1)
    def _():
        epilogue(acc_ref[...] + part())


def _rotate_pairs(y, cos, sin_lo, sin_hi, half):
    return (y * cos
            + pltpu.roll(y, LANES - half, 1) * sin_lo
            + pltpu.roll(y, half, 1) * sin_hi)


def _gemm_relu2_kernel(a_ref, w_ref, o_ref, *scratch, nk):
    def epilogue(acc):
        o_ref[...] = jnp.square(jnp.maximum(acc, 0.0)).astype(o_ref.dtype)
    _accumulate(a_ref, w_ref, scratch[0] if scratch else None, nk, epilogue)


def _gemm_resid_kernel(a_ref, w_ref, x_ref, gate_ref, o_ref, *scratch, nk, alpha):
    def epilogue(acc):
        o_ref[...] = alpha * x_ref[...] + gate_ref[...] * acc
    _accumulate(a_ref, w_ref, scratch[0] if scratch else None, nk, epilogue)


def _gemm_proj_a_kernel(a_ref, w_ref, cos_ref, slo_ref, shi_ref, o_ref, *scratch, nk, rope_tiles):
    heads = o_ref.shape[0]

    def epilogue(acc):
        j = pl.program_id(1)

        @pl.when(j < rope_tiles)
        def _():
            for hh in range(heads):
                y = acc[:, hh * HEAD_DIM:(hh + 1) * HEAD_DIM]
                o_ref[hh] = _rotate_pairs(y, cos_ref[...], slo_ref[...], shi_ref[...], PARTIAL_ROT_DIM // 2)

        @pl.when(j >= rope_tiles)
        def _():
            for hh in range(heads):
                o_ref[hh] = acc[:, hh * HEAD_DIM:(hh + 1) * HEAD_DIM]

    _accumulate(a_ref, w_ref, scratch[0] if scratch else None, nk, epilogue)


def _axial_normed_rotated(y, g_ref, cos_ref, slo_ref, shi_ref):
    return _rotate_pairs(_rms_rows(y, g_ref[...]), cos_ref[...], slo_ref[...], shi_ref[...], HEAD_DIM // 4)


def _gemm_proj_bq_kernel(a_ref, w_ref, cos_ref, slo_ref, shi_ref, gq_ref, o_ref, *scratch, nk):
    def epilogue(acc):
        for hh in range(o_ref.shape[0]):
            y = acc[:, hh * HEAD_DIM:(hh + 1) * HEAD_DIM]
            q = _axial_normed_rotated(y, gq_ref, cos_ref, slo_ref, shi_ref)
            o_ref[hh] = (q * (ATTN_SCALE * LOG2_E)).astype(o_ref.dtype)

    _accumulate(a_ref, w_ref, scratch[0] if scratch else None, nk, epilogue)


def _gemm_proj_bkv_kernel(a_ref, w_ref, cos_ref, slo_ref, shi_ref, gk_ref, k_ref, vt_ref, *scratch, nk):
    def epilogue(acc):
        for hh in range(N_KV_B):
            y = acc[:, hh * HEAD_DIM:(hh + 1) * HEAD_DIM]
            k_ref[hh] = _axial_normed_rotated(y, gk_ref, cos_ref, slo_ref, shi_ref).astype(k_ref.dtype)
        for hh in range(N_KV_B):
            y = acc[:, (N_KV_B + hh) * HEAD_DIM:(N_KV_B + hh + 1) * HEAD_DIM]
            vt_ref[hh] = y.T.astype(vt_ref.dtype)

    _accumulate(a_ref, w_ref, scratch[0] if scratch else None, nk, epilogue)


def _gemm_tiles(m, n, k, seq):
    tm = min(1024, seq, m)
    tn = min(1024, n)
    tk = k if k <= 4096 else 2048
    return tm, tn, tk


def _gemm_call(kernel_fn, a, w, extra_inputs, extra_specs, out_shape, out_spec, tiles, n_cols, col_offset, name):
    w_all, layer = w
    m, k = a.shape
    tm, tn, tk = tiles
    nk = k // tk
    joff = col_offset // tn
    scratch = [pltpu.VMEM((tm, tn), F32)] if nk > 1 else []
    return pl.pallas_call(
        functools.partial(kernel_fn, nk=nk),
        out_shape=out_shape,
        grid=(m // tm, n_cols // tn, nk),
        in_specs=[
            pl.BlockSpec((tm, tk), lambda i, j, kk: (i, kk)),
            pl.BlockSpec((None, tk, tn), lambda i, j, kk: (layer, kk, j + joff)),
        ] + extra_specs,
        out_specs=out_spec,
        scratch_shapes=scratch,
        compiler_params=_params(("parallel", "parallel", "arbitrary")),
        name=name,
    )(a, w_all, *extra_inputs)


def _gemm_relu2(a, w):
    m, k = a.shape
    n = w[0].shape[2]
    tiles = _gemm_tiles(m, n, k, m)
    tm, tn, _ = tiles
    return _gemm_call(
        _gemm_relu2_kernel, a, w, [], [],
        jax.ShapeDtypeStruct((m, n), BF16),
        pl.BlockSpec((tm, tn), lambda i, j, kk: (i, j)),
        tiles, n, 0, "gemm_relu2")


def _gemm_resid(a, w, x, gate, seq, alpha):
    m, k = a.shape
    n = w[0].shape[2]
    tiles = _gemm_tiles(m, n, k, seq)
    tm, tn, _ = tiles
    return _gemm_call(
        functools.partial(_gemm_resid_kernel, alpha=alpha), a, w, [x, gate],
        [pl.BlockSpec((tm, tn), lambda i, j, kk: (i, j)),
         pl.BlockSpec((None, 1, tn), lambda i, j, kk: ((i * tm) // seq, 0, j))],
        jax.ShapeDtypeStruct((m, n), F32),
        pl.BlockSpec((tm, tn), lambda i, j, kk: (i, j)),
        tiles, n, 0, "gemm_resid")


def _rope_specs(tm, seq):
    nrow = seq // tm
    return [pl.BlockSpec((tm, LANES), lambda i, j, kk: (i % nrow, 0))] * 3


def _gemm_proj_a(h, w_in, tables, seq):
    m, k = h.shape
    tiles = _gemm_tiles(m, PROJ_A, k, seq)
    tm, tn, _ = tiles
    heads = tn // HEAD_DIM
    return _gemm_call(
        functools.partial(_gemm_proj_a_kernel, rope_tiles=2 * WIDTH_A // tn), h, w_in, list(tables),
        _rope_specs(tm, seq),
        jax.ShapeDtypeStruct((PROJ_A // HEAD_DIM, m, HEAD_DIM), F32),
        pl.BlockSpec((heads, tm, HEAD_DIM), lambda i, j, kk: (j, i, 0)),
        tiles, PROJ_A, 0, "gemm_proj_a")


def _gemm_proj_bq(h, w_in, tables, g_q, seq):
    m, k = h.shape
    tiles = _gemm_tiles(m, WIDTH_B, k, seq)
    tm, tn, _ = tiles
    heads = tn // HEAD_DIM
    return _gemm_call(
        _gemm_proj_bq_kernel, h, w_in, list(tables) + [g_q.reshape(1, HEAD_DIM)],
        _rope_specs(tm, seq) + [pl.BlockSpec((1, HEAD_DIM), lambda i, j, kk: (0, 0))],
        jax.ShapeDtypeStruct((N_HEADS_B, m, HEAD_DIM), BF16),
        pl.BlockSpec((heads, tm, HEAD_DIM), lambda i, j, kk: (j, i, 0)),
        tiles, WIDTH_B, PROJ_A, "gemm_proj_bq")


def _gemm_proj_bkv(h, w_in, tables, g_k, seq):
    m, k = h.shape
    tm, _, tk = _gemm_tiles(m, 2 * KV_WIDTH_B, k, seq)
    tiles = (tm, 2 * KV_WIDTH_B, tk)
    return _gemm_call(
        _gemm_proj_bkv_kernel, h, w_in, list(tables) + [g_k.reshape(1, HEAD_DIM)],
        _rope_specs(tm, seq) + [pl.BlockSpec((1, HEAD_DIM), lambda i, j, kk: (0, 0))],
        (jax.ShapeDtypeStruct((N_KV_B, m, HEAD_DIM), BF16), jax.ShapeDtypeStruct((N_KV_B, HEAD_DIM, m), BF16)),
        (pl.BlockSpec((N_KV_B, tm, HEAD_DIM), lambda i, j, kk: (0, i, 0)),
         pl.BlockSpec((N_KV_B, HEAD_DIM, tm), lambda i, j, kk: (0, 0, i))),
        tiles, 2 * KV_WIDTH_B, PROJ_A + WIDTH_B, "gemm_proj_bkv")


def _mixer_a_kernel(q_ref, k_ref, v_ref, o_ref, acc_ref, m_ref, d_ref, *, seq, window, plan):
    win = pl.program_id(2)

    for branch, (dil, tq) in enumerate(plan):
        length = seq // dil
        tkv = tq + 2 * HALF_WINDOW
        blocks = window // (dil * tq)
        offset = (lax.broadcasted_iota(jnp.int32, (tq, tkv), 1)
                  - lax.broadcasted_iota(jnp.int32, (tq, tkv), 0))

        def rows(start, size, dil=dil):
            return pl.ds(start, size) if dil == 1 else pl.ds(start, size, stride=dil)

        for res in range(dil):
            for blk in range(blocks):
                loc = res + dil * tq * blk
                q0 = win * (window // dil) + tq * blk
                k0 = jnp.clip(q0 - HALF_WINDOW, 0, length - tkv)
                q = q_ref[rows(loc, tq), :].astype(BF16)
                k = k_ref[rows(res + dil * k0, tkv), :].astype(BF16)
                v = v_ref[rows(res + dil * k0, tkv), :].astype(BF16)
                s = lax.dot_general(q, k, (((1,), (1,)), ((), ())), preferred_element_type=F32) * ATTN_SCALE
                shifted = offset + (k0 - q0 + HALF_WINDOW)
                valid = lax.bitcast_convert_type(shifted, jnp.uint32) <= 2 * HALF_WINDOW
                s = jnp.where(valid, s, NEG_BIG)
                m = jnp.max(s, axis=-1, keepdims=True)
                p = jnp.exp(s - m)
                den = jnp.sum(p, axis=-1, keepdims=True)
                acc_ref[branch, rows(loc, tq), :] = jnp.dot(p.astype(BF16), v, preferred_element_type=F32)
                m_ref[branch, rows(loc, tq), :] = jnp.broadcast_to(m, (tq, LANES))
                d_ref[branch, rows(loc, tq), :] = jnp.broadcast_to(den, (tq, LANES))

    m_all = m_ref[0]
    for branch in range(1, len(plan)):
        m_all = jnp.maximum(m_all, m_ref[branch])
    num = jnp.zeros(o_ref.shape, F32)
    tot = jnp.zeros(o_ref.shape, F32)
    for branch in range(len(plan)):
        w = jnp.exp(m_ref[branch] - m_all)
        num = num + w * acc_ref[branch]
        tot = tot + w * d_ref[branch]
    o_ref[...] = num / tot


def _mixer_a_plan(seq, window):
    plan = []
    for dil in DILATIONS:
        length = seq // dil
        tq = min(128, length // 2, window // dil)
        assert length >= tq + 2 * HALF_WINDOW and window % (dil * tq) == 0, (seq, window, dil)
        plan.append((dil, tq))
    return tuple(plan)


def _mixer_a(qkv, batch, seq):
    rows = batch * seq
    window = min(2048, seq)
    nwin = seq // window
    plan = _mixer_a_plan(seq, window)
    q_spec = pl.BlockSpec((None, window, HEAD_DIM), lambda b, h, w: (h, b * nwin + w, 0))
    k_spec = pl.BlockSpec((None, seq, HEAD_DIM), lambda b, h, w: (N_HEADS_A + h, b, 0))
    v_spec = pl.BlockSpec((None, seq, HEAD_DIM), lambda b, h, w: (2 * N_HEADS_A + h, b, 0))
    stat = pltpu.VMEM((len(plan), window, LANES), F32)
    return pl.pallas_call(
        functools.partial(_mixer_a_kernel, seq=seq, window=window, plan=plan),
        out_shape=jax.ShapeDtypeStruct((rows, WIDTH_A), F32),
        grid=(batch, N_HEADS_A, nwin),
        in_specs=[q_spec, k_spec, v_spec],
        out_specs=pl.BlockSpec((window, HEAD_DIM), lambda b, h, w: (b * nwin + w, h)),
        scratch_shapes=[stat, stat, stat],
        compiler_params=_params(("parallel", "parallel", "arbitrary")),
        name="mixer_a_dilated",
    )(qkv, qkv, qkv)


SUBLANES = 8
MIXER_B_CHAIN_COLS = 256


def _reduce_rows(x, op):
    parts = [x[r:r + SUBLANES] for r in range(0, x.shape[0], SUBLANES)]
    while len(parts) > 1:
        parts = [op(parts[i], parts[i + 1]) for i in range(0, len(parts) - 1, 2)] + (
            [parts[-1]] if len(parts) % 2 else [])
    top = parts[0]
    rows = [top[r:r + 1] for r in range(top.shape[0])]
    while len(rows) > 1:
        rows = [op(rows[i], rows[i + 1]) for i in range(0, len(rows), 2)]
    return rows[0]


def _mixer_b_kernel(q_ref, k_ref, vt_ref, o_ref, m_ref, l_ref, acc_ref, *, nkv, chains):
    ki = pl.program_id(3)

    @pl.when(ki == 0)
    def _():
        m_ref[...] = jnp.full_like(m_ref, NEG_BIG)
        l_ref[...] = jnp.zeros_like(l_ref)
        acc_ref[...] = jnp.zeros_like(acc_ref)

    k = k_ref[...]
    vt = vt_ref[...]
    tq = q_ref.shape[1]
    cols = m_ref.shape[-1]
    chains_per_head = tq // cols

    def scores_and_max(c):
        q = q_ref[c // chains_per_head, pl.ds((c % chains_per_head) * cols, cols), :]
        st = lax.dot_general(k, q, (((1,), (1,)), ((), ())), preferred_element_type=F32)
        return st, jnp.maximum(m_ref[c], _reduce_rows(st, jnp.maximum))

    def accumulate(c, st, m_new):
        m_old = m_ref[c]
        alpha = jnp.exp2(m_old - m_new)
        p = jnp.exp2(st - m_new)
        l_ref[c] = alpha * l_ref[c] + _reduce_rows(p, jnp.add)
        acc_ref[c] = alpha * acc_ref[c] + jnp.dot(vt, p.astype(BF16), preferred_element_type=F32)
        m_ref[c] = m_new

    pending = [scores_and_max(c) for c in range(chains)]
    for c in range(chains):
        accumulate(c, *pending[c])

    @pl.when(ki == nkv - 1)
    def _():
        for c in range(chains):
            head, part = divmod(c, chains_per_head)
            o_ref[part * cols:(part + 1) * cols, head * HEAD_DIM:(head + 1) * HEAD_DIM] = (
                acc_ref[c] / l_ref[c]).T


def _mixer_b(q, k, vt, batch, seq):
    rows = batch * seq
    tq = min(1024, seq)
    tkv = min(1024, seq)
    nq, nkv = seq // tq, seq // tkv
    cols = min(MIXER_B_CHAIN_COLS, tq)
    chains = GQA_GROUP * tq // cols
    q_spec = pl.BlockSpec((GQA_GROUP, tq, HEAD_DIM), lambda b, g, qi, ki: (g, b * nq + qi, 0))
    k_spec = pl.BlockSpec((None, tkv, HEAD_DIM), lambda b, g, qi, ki: (g, b * nkv + ki, 0))
    vt_spec = pl.BlockSpec((None, HEAD_DIM, tkv), lambda b, g, qi, ki: (g, 0, b * nkv + ki))
    return pl.pallas_call(
        functools.partial(_mixer_b_kernel, nkv=nkv, chains=chains),
        out_shape=jax.ShapeDtypeStruct((rows, WIDTH_B), F32),
        grid=(batch, N_KV_B, nq, nkv),
        in_specs=[q_spec, k_spec, vt_spec],
        out_specs=pl.BlockSpec((tq, GQA_GROUP * HEAD_DIM), lambda b, g, qi, ki: (b * nq + qi, g)),
        scratch_shapes=[
            pltpu.VMEM((chains, 1, cols), F32),
            pltpu.VMEM((chains, 1, cols), F32),
            pltpu.VMEM((chains, HEAD_DIM, cols), F32),
        ],
        compiler_params=_params(("parallel", "parallel", "parallel", "arbitrary")),
        name="mixer_b_gqa",
    )(q, k, vt)


def _rotary_tables(angle_groups):
    seq = angle_groups[0].shape[0]
    cos = np.ones((seq, HEAD_DIM), np.float64)
    sin_lo = np.zeros((seq, HEAD_DIM), np.float64)
    sin_hi = np.zeros((seq, HEAD_DIM), np.float64)
    lane = 0
    for ang in angle_groups:
        half = ang.shape[1]
        cos[:, lane:lane + half] = np.cos(ang)
        cos[:, lane + half:lane + 2 * half] = np.cos(ang)
        sin_lo[:, lane:lane + half] = -np.sin(ang)
        sin_hi[:, lane + half:lane + 2 * half] = np.sin(ang)
        lane += 2 * half
    return tuple(jnp.asarray(t, F32) for t in (cos, sin_lo, sin_hi))


def _inv_freq(rot_dim, theta):
    return theta ** (-np.arange(0, rot_dim, 2, dtype=np.float64) / rot_dim)


def _partial_rope_tables(seq):
    pos = np.arange(seq, dtype=np.float64)
    return _rotary_tables([pos[:, None] * _inv_freq(PARTIAL_ROT_DIM, ROPE_THETA)[None, :]])


def _axial_rope_tables(seq):
    pos = np.arange(seq)
    inv = _inv_freq(HEAD_DIM // 2, AXIAL_THETA)[None, :]
    row = (pos // GRID_W).astype(np.float64)[:, None]
    col = (pos % GRID_W).astype(np.float64)[:, None]
    return _rotary_tables([row * inv, col * inv])


def _run_group(x, mod, weights, depth):
    batch, seq, d = x.shape
    rows = batch * seq
    alpha = (2 * depth) ** 0.25
    tables_a = _partial_rope_tables(seq)
    tables_b = _axial_rope_tables(seq)
    x = x.reshape(rows, d)

    def chunks(l):
        m = mod[l].reshape(batch, 6, 1, d)
        return [m[:, i] for i in range(6)]

    sh1, sc1, g1, sh2, sc2, g2 = chunks(0)
    h = _modulate(x, sc1, sh1, seq)
    for l in range(depth):
        w = weights[l]
        qkv_a = _gemm_proj_a(h, w["w_in"], tables_a, seq)
        q_b = _gemm_proj_bq(h, w["w_in"], tables_b, w["g_q"], seq)
        k_b, vt_b = _gemm_proj_bkv(h, w["w_in"], tables_b, w["g_k"], seq)
        o_a = _mixer_a(qkv_a, batch, seq)
        o_b = _mixer_b(q_b, k_b, vt_b, batch, seq)
        mixed = _norm_mix(o_a, o_b, w["g_out_a"], w["g_out_b"], seq)
        y = _gemm_resid(mixed, w["w_out"], x, g1, seq, alpha)
        x, h = _layernorm_modulate(y, w["ln1_g"], w["ln1_b"], sc2, sh2, seq)
        f = _gemm_relu2(h, w["w_up"])
        y = _gemm_resid(f, w["w_down"], x, g2, seq, alpha)
        if l + 1 < depth:
            sh1, sc1, g1, sh2, sc2, g2 = chunks(l + 1)
            x, h = _layernorm_modulate(y, w["ln2_g"], w["ln2_b"], sc1, sh1, seq)
        else:
            x = _layernorm(y, w["ln2_g"], w["ln2_b"], seq)
    return x.reshape(batch, seq, d)


def kernel(x_prompt, x_sample, c_prompt, c_sample, w_ada, b_ada, w_in, g_q, g_k, g_out_a, g_out_b, w_out,
           ln1_g, ln1_b, w_up, w_down, ln2_g, ln2_b):
    depth = w_ada.shape[0]
    n_prompt, n_sample = c_prompt.shape[0], c_sample.shape[0]
    pad = -(n_prompt + n_sample) % 8
    c_rows = jnp.concatenate([c_prompt, c_sample, jnp.zeros((pad, c_prompt.shape[1]), F32)], axis=0)
    mod = _ada(c_rows, w_ada, b_ada)

    w_in, w_out, w_up, w_down = (w.astype(BF16) for w in (w_in, w_out, w_up, w_down))
    weights = [
        dict(w_in=(w_in, l), w_out=(w_out, l), w_up=(w_up, l), w_down=(w_down, l),
             g_q=g_q[l], g_k=g_k[l], g_out_a=g_out_a[l], g_out_b=g_out_b[l],
             ln1_g=ln1_g[l], ln1_b=ln1_b[l], ln2_g=ln2_g[l], ln2_b=ln2_b[l])
        for l in range(depth)
    ]
    y_prompt = _run_group(x_prompt, mod[:, :n_prompt], weights, depth)
    y_sample = _run_group(x_sample, mod[:, n_prompt:n_prompt + n_sample], weights, depth)
    return (y_prompt, y_sample)
```

```python
import functools

import numpy as np
import jax
import jax.numpy as jnp
from jax import lax
from jax.experimental import pallas as pl
from jax.experimental.pallas import tpu as pltpu

HEAD_DIM = 128
N_HEADS_A = 16
N_HEADS_B = 16
N_KV_B = 4
GQA_GROUP = N_HEADS_B // N_KV_B
WIDTH_A = N_HEADS_A * HEAD_DIM
WIDTH_B = N_HEADS_B * HEAD_DIM
KV_WIDTH_B = N_KV_B * HEAD_DIM
PROJ_A = 3 * WIDTH_A
PROJ_B = WIDTH_B + 2 * KV_WIDTH_B
DILATIONS = (1, 4, 16)
HALF_WINDOW = 64
PARTIAL_ROT_DIM = HEAD_DIM // 4
ROPE_THETA = 500000.0
AXIAL_THETA = 10000.0
GRID_W = 64
LN_EPS = 1e-5
RMS_EPS = 1e-6
NEG_BIG = -1e30
ATTN_SCALE = HEAD_DIM ** -0.5
LOG2_E = 1.4426950408889634

V7X_VMEM_BYTES = 64 * 1024 * 1024
VMEM_LIMIT_BYTES = V7X_VMEM_BYTES - 6 * 1024 * 1024
LANES = 128

F32 = jnp.float32
BF16 = jnp.bfloat16


def _params(semantics):
    return pltpu.CompilerParams(dimension_semantics=semantics, vmem_limit_bytes=VMEM_LIMIT_BYTES)


def _ada_kernel(c_ref, w_ref, b_ref, o_ref):
    c = c_ref[...]
    s = (c * (1.0 / (1.0 + jnp.exp(-c)))).astype(BF16)
    w = w_ref[...].astype(BF16)
    o_ref[...] = jnp.dot(s, w, preferred_element_type=F32) + b_ref[...]


def _ada(c_rows, w_ada, b_ada):
    depth, d, n = w_ada.shape
    rows = c_rows.shape[0]
    tn = min(512, n)
    return pl.pallas_call(
        _ada_kernel,
        out_shape=jax.ShapeDtypeStruct((depth, rows, n), F32),
        grid=(depth, n // tn),
        in_specs=[
            pl.BlockSpec((rows, d), lambda l, j: (0, 0)),
            pl.BlockSpec((None, d, tn), lambda l, j: (l, 0, j)),
            pl.BlockSpec((None, 1, tn), lambda l, j: (l, 0, j)),
        ],
        out_specs=pl.BlockSpec((None, rows, tn), lambda l, j: (l, 0, j)),
        compiler_params=_params(("parallel", "parallel")),
        name="ada_modulation",
    )(c_rows, w_ada, b_ada.reshape(depth, 1, n))


def _modulate_kernel(x_ref, sc_ref, sh_ref, h_ref):
    h_ref[...] = (x_ref[...] * (1.0 + sc_ref[...]) + sh_ref[...]).astype(BF16)


def _layernorm_rows(y, g, b):
    mu = jnp.mean(y, axis=-1, keepdims=True)
    yc = y - mu
    var = jnp.mean(yc * yc, axis=-1, keepdims=True)
    return yc * lax.rsqrt(var + LN_EPS) * g + b


def _ln_kernel(y_ref, g_ref, b_ref, x_ref):
    x_ref[...] = _layernorm_rows(y_ref[...], g_ref[...], b_ref[...])


def _ln_modulate_kernel(y_ref, g_ref, b_ref, sc_ref, sh_ref, x_ref, h_ref):
    x = _layernorm_rows(y_ref[...], g_ref[...], b_ref[...])
    x_ref[...] = x
    h_ref[...] = (x * (1.0 + sc_ref[...]) + sh_ref[...]).astype(BF16)


def _rms_rows(o, g):
    return o * lax.rsqrt(jnp.mean(o * o, axis=-1, keepdims=True) + RMS_EPS) * g


def _norm_mix_kernel(oa_ref, ob_ref, ga_ref, gb_ref, m_ref):
    wa = oa_ref.shape[-1]
    m_ref[:, :wa] = _rms_rows(oa_ref[...], ga_ref[...]).astype(BF16)
    m_ref[:, wa:] = _rms_rows(ob_ref[...], gb_ref[...]).astype(BF16)


def _row_tile(rows, seq):
    return min(256, seq, rows)


def _modulate(x, sc, sh, seq):
    rows, d = x.shape
    tm = _row_tile(rows, seq)
    per_batch = lambda i: ((i * tm) // seq, 0, 0)
    return pl.pallas_call(
        _modulate_kernel,
        out_shape=jax.ShapeDtypeStruct((rows, d), BF16),
        grid=(rows // tm,),
        in_specs=[
            pl.BlockSpec((tm, d), lambda i: (i, 0)),
            pl.BlockSpec((None, 1, d), per_batch),
            pl.BlockSpec((None, 1, d), per_batch),
        ],
        out_specs=pl.BlockSpec((tm, d), lambda i: (i, 0)),
        compiler_params=_params(("parallel",)),
        name="modulate",
    )(x, sc, sh)


def _layernorm(y, g, b, seq):
    rows, d = y.shape
    tm = _row_tile(rows, seq)
    return pl.pallas_call(
        _ln_kernel,
        out_shape=jax.ShapeDtypeStruct((rows, d), F32),
        grid=(rows // tm,),
        in_specs=[
            pl.BlockSpec((tm, d), lambda i: (i, 0)),
            pl.BlockSpec((1, d), lambda i: (0, 0)),
            pl.BlockSpec((1, d), lambda i: (0, 0)),
        ],
        out_specs=pl.BlockSpec((tm, d), lambda i: (i, 0)),
        compiler_params=_params(("parallel",)),
        name="layernorm",
    )(y, g.reshape(1, d), b.reshape(1, d))


def _layernorm_modulate(y, g, b, sc, sh, seq):
    rows, d = y.shape
    tm = _row_tile(rows, seq)
    per_batch = lambda i: ((i * tm) // seq, 0, 0)
    return pl.pallas_call(
        _ln_modulate_kernel,
        out_shape=(jax.ShapeDtypeStruct((rows, d), F32), jax.ShapeDtypeStruct((rows, d), BF16)),
        grid=(rows // tm,),
        in_specs=[
            pl.BlockSpec((tm, d), lambda i: (i, 0)),
            pl.BlockSpec((1, d), lambda i: (0, 0)),
            pl.BlockSpec((1, d), lambda i: (0, 0)),
            pl.BlockSpec((None, 1, d), per_batch),
            pl.BlockSpec((None, 1, d), per_batch),
        ],
        out_specs=(pl.BlockSpec((tm, d), lambda i: (i, 0)), pl.BlockSpec((tm, d), lambda i: (i, 0))),
        compiler_params=_params(("parallel",)),
        name="layernorm_modulate",
    )(y, g.reshape(1, d), b.reshape(1, d), sc, sh)


def _norm_mix(o_a, o_b, g_a, g_b, seq):
    rows, wa = o_a.shape
    wb = o_b.shape[1]
    tm = _row_tile(rows, seq)
    return pl.pallas_call(
        _norm_mix_kernel,
        out_shape=jax.ShapeDtypeStruct((rows, wa + wb), BF16),
        grid=(rows // tm,),
        in_specs=[
            pl.BlockSpec((tm, wa), lambda i: (i, 0)),
            pl.BlockSpec((tm, wb), lambda i: (i, 0)),
            pl.BlockSpec((1, wa), lambda i: (0, 0)),
            pl.BlockSpec((1, wb), lambda i: (0, 0)),
        ],
        out_specs=pl.BlockSpec((tm, wa + wb), lambda i: (i, 0)),
        compiler_params=_params(("parallel",)),
        name="norm_mix",
    )(o_a, o_b, g_a.reshape(1, wa), g_b.reshape(1, wb))


def _accumulate(a_ref, w_ref, acc_ref, nk, epilogue):
    def part():
        return jnp.dot(a_ref[...], w_ref[...], preferred_element_type=F32)

    if nk == 1:
        epilogue(part())
        return
    k = pl.program_id(2)

    @pl.when(k == 0)
    def _():
        acc_ref[...] = part()

    @pl.when(jnp.logical_and(k > 0, k < nk - 1))
    def _():
        acc_ref[...] += part()

    @pl.when(k == nk - 1)
    def _():
        epilogue(acc_ref[...] + part())


def _rotate_pairs(y, cos, sin_lo, sin_hi, half):
    return (y * cos
            + pltpu.roll(y, LANES - half, 1) * sin_lo
            + pltpu.roll(y, half, 1) * sin_hi)


def _gemm_relu2_kernel(a_ref, w_ref, o_ref, *scratch, nk):
    def epilogue(acc):
        o_ref[...] = jnp.square(jnp.maximum(acc, 0.0)).astype(o_ref.dtype)
    _accumulate(a_ref, w_ref, scratch[0] if scratch else None, nk, epilogue)


def _gemm_resid_kernel(a_ref, w_ref, x_ref, gate_ref, o_ref, *scratch, nk, alpha):
    def epilogue(acc):
        o_ref[...] = alpha * x_ref[...] + gate_ref[...] * acc
    _accumulate(a_ref, w_ref, scratch[0] if scratch else None, nk, epilogue)


def _gemm_proj_a_kernel(a_ref, w_ref, cos_ref, slo_ref, shi_ref, o_ref, *scratch, nk, rope_tiles):
    heads = o_ref.shape[0]

    def epilogue(acc):
        j = pl.program_id(1)

        @pl.when(j < rope_tiles)
        def _():
            for hh in range(heads):
                y = acc[:, hh * HEAD_DIM:(hh + 1) * HEAD_DIM]
                o_ref[hh] = _rotate_pairs(y, cos_ref[...], slo_ref[...], shi_ref[...], PARTIAL_ROT_DIM // 2)

        @pl.when(j >= rope_tiles)
        def _():
            for hh in range(heads):
                o_ref[hh] = acc[:, hh * HEAD_DIM:(hh + 1) * HEAD_DIM]

    _accumulate(a_ref, w_ref, scratch[0] if scratch else None, nk, epilogue)


def _axial_normed_rotated(y, g_ref, cos_ref, slo_ref, shi_ref):
    return _rotate_pairs(_rms_rows(y, g_ref[...]), cos_ref[...], slo_ref[...], shi_ref[...], HEAD_DIM // 4)


def _gemm_proj_bq_kernel(a_ref, w_ref, cos_ref, slo_ref, shi_ref, gq_ref, o_ref, *scratch, nk):
    def epilogue(acc):
        for hh in range(o_ref.shape[0]):
            y = acc[:, hh * HEAD_DIM:(hh + 1) * HEAD_DIM]
            q = _axial_normed_rotated(y, gq_ref, cos_ref, slo_ref, shi_ref)
            o_ref[hh] = (q * (ATTN_SCALE * LOG2_E)).astype(o_ref.dtype)

    _accumulate(a_ref, w_ref, scratch[0] if scratch else None, nk, epilogue)


def _gemm_proj_bkv_kernel(a_ref, w_ref, cos_ref, slo_ref, shi_ref, gk_ref, k_ref, vt_ref, *scratch, nk):
    def epilogue(acc):
        for hh in range(N_KV_B):
            y = acc[:, hh * HEAD_DIM:(hh + 1) * HEAD_DIM]
            k_ref[hh] = _axial_normed_rotated(y, gk_ref, cos_ref, slo_ref, shi_ref).astype(k_ref.dtype)
        for hh in range(N_KV_B):
            y = acc[:, (N_KV_B + hh) * HEAD_DIM:(N_KV_B + hh + 1) * HEAD_DIM]
            vt_ref[hh] = y.T.astype(vt_ref.dtype)

    _accumulate(a_ref, w_ref, scratch[0] if scratch else None, nk, epilogue)


def _gemm_tiles(m, n, k, seq):
    tm = min(1024, seq, m)
    tn = min(1024, n)
    tk = k if k <= 4096 else 2048
    return tm, tn, tk


def _gemm_call(kernel_fn, a, w, extra_inputs, extra_specs, out_shape, out_spec, tiles, n_cols, col_offset, name):
    w_all, layer = w
    m, k = a.shape
    tm, tn, tk = tiles
    nk = k // tk
    joff = col_offset // tn
    scratch = [pltpu.VMEM((tm, tn), F32)] if nk > 1 else []
    return pl.pallas_call(
        functools.partial(kernel_fn, nk=nk),
        out_shape=out_shape,
        grid=(m // tm, n_cols // tn, nk),
        in_specs=[
            pl.BlockSpec((tm, tk), lambda i, j, kk: (i, kk)),
            pl.BlockSpec((None, tk, tn), lambda i, j, kk: (layer, kk, j + joff)),
        ] + extra_specs,
        out_specs=out_spec,
        scratch_shapes=scratch,
        compiler_params=_params(("parallel", "parallel", "arbitrary")),
        name=name,
    )(a, w_all, *extra_inputs)


def _gemm_relu2(a, w):
    m, k = a.shape
    n = w[0].shape[2]
    tiles = _gemm_tiles(m, n, k, m)
    tm, tn, _ = tiles
    return _gemm_call(
        _gemm_relu2_kernel, a, w, [], [],
        jax.ShapeDtypeStruct((m, n), BF16),
        pl.BlockSpec((tm, tn), lambda i, j, kk: (i, j)),
        tiles, n, 0, "gemm_relu2")


def _gemm_resid(a, w, x, gate, seq, alpha):
    m, k = a.shape
    n = w[0].shape[2]
    tiles = _gemm_tiles(m, n, k, seq)
    tm, tn, _ = tiles
    return _gemm_call(
        functools.partial(_gemm_resid_kernel, alpha=alpha), a, w, [x, gate],
        [pl.BlockSpec((tm, tn), lambda i, j, kk: (i, j)),
         pl.BlockSpec((None, 1, tn), lambda i, j, kk: ((i * tm) // seq, 0, j))],
        jax.ShapeDtypeStruct((m, n), F32),
        pl.BlockSpec((tm, tn), lambda i, j, kk: (i, j)),
        tiles, n, 0, "gemm_resid")


def _rope_specs(tm, seq):
    nrow = seq // tm
    return [pl.BlockSpec((tm, LANES), lambda i, j, kk: (i % nrow, 0))] * 3


def _gemm_proj_a(h, w_in, tables, seq):
    m, k = h.shape
    tiles = _gemm_tiles(m, PROJ_A, k, seq)
    tm, tn, _ = tiles
    heads = tn // HEAD_DIM
    return _gemm_call(
        functools.partial(_gemm_proj_a_kernel, rope_tiles=2 * WIDTH_A // tn), h, w_in, list(tables),
        _rope_specs(tm, seq),
        jax.ShapeDtypeStruct((PROJ_A // HEAD_DIM, m, HEAD_DIM), F32),
        pl.BlockSpec((heads, tm, HEAD_DIM), lambda i, j, kk: (j, i, 0)),
        tiles, PROJ_A, 0, "gemm_proj_a")


def _gemm_proj_bq(h, w_in, tables, g_q, seq):
    m, k = h.shape
    tiles = _gemm_tiles(m, WIDTH_B, k, seq)
    tm, tn, _ = tiles
    heads = tn // HEAD_DIM
    return _gemm_call(
        _gemm_proj_bq_kernel, h, w_in, list(tables) + [g_q.reshape(1, HEAD_DIM)],
        _rope_specs(tm, seq) + [pl.BlockSpec((1, HEAD_DIM), lambda i, j, kk: (0, 0))],
        jax.ShapeDtypeStruct((N_HEADS_B, m, HEAD_DIM), BF16),
        pl.BlockSpec((heads, tm, HEAD_DIM), lambda i, j, kk: (j, i, 0)),
        tiles, WIDTH_B, PROJ_A, "gemm_proj_bq")


def _gemm_proj_bkv(h, w_in, tables, g_k, seq):
    m, k = h.shape
    tm, _, tk = _gemm_tiles(m, 2 * KV_WIDTH_B, k, seq)
    tiles = (tm, 2 * KV_WIDTH_B, tk)
    return _gemm_call(
        _gemm_proj_bkv_kernel, h, w_in, list(tables) + [g_k.reshape(1, HEAD_DIM)],
        _rope_specs(tm, seq) + [pl.BlockSpec((1, HEAD_DIM), lambda i, j, kk: (0, 0))],
        (jax.ShapeDtypeStruct((N_KV_B, m, HEAD_DIM), BF16), jax.ShapeDtypeStruct((N_KV_B, HEAD_DIM, m), BF16)),
        (pl.BlockSpec((N_KV_B, tm, HEAD_DIM), lambda i, j, kk: (0, i, 0)),
         pl.BlockSpec((N_KV_B, HEAD_DIM, tm), lambda i, j, kk: (0, 0, i))),
        tiles, 2 * KV_WIDTH_B, PROJ_A + WIDTH_B, "gemm_proj_bkv")


def _mixer_a_kernel(q_ref, k_ref, v_ref, o_ref, acc_ref, m_ref, d_ref, *, seq, window, plan):
    win = pl.program_id(2)

    for branch, (dil, tq) in enumerate(plan):
        length = seq // dil
        tkv = tq + 2 * HALF_WINDOW
        blocks = window // (dil * tq)
        offset = (lax.broadcasted_iota(jnp.int32, (tq, tkv), 1)
                  - lax.broadcasted_iota(jnp.int32, (tq, tkv), 0))

        def rows(start, size, dil=dil):
            return pl.ds(start, size) if dil == 1 else pl.ds(start, size, stride=dil)

        for res in range(dil):
            for blk in range(blocks):
                loc = res + dil * tq * blk
                q0 = win * (window // dil) + tq * blk
                k0 = jnp.clip(q0 - HALF_WINDOW, 0, length - tkv)
                q = (q_ref[rows(loc, tq), :] * (ATTN_SCALE * LOG2_E)).astype(BF16)
                k = k_ref[rows(res + dil * k0, tkv), :].astype(BF16)
                v = v_ref[rows(res + dil * k0, tkv), :].astype(BF16)
                s = lax.dot_general(q, k, (((1,), (1,)), ((), ())), preferred_element_type=F32)
                shifted = offset + (k0 - q0 + HALF_WINDOW)
                valid = lax.bitcast_convert_type(shifted, jnp.uint32) <= 2 * HALF_WINDOW
                s = jnp.where(valid, s, NEG_BIG)
                m = jnp.max(s, axis=-1, keepdims=True)
                p = jnp.exp2(s - m).astype(BF16)
                v_ones = jnp.concatenate([v, jnp.ones((tkv, LANES), BF16)], axis=1)
                pv = jnp.dot(p, v_ones, preferred_element_type=F32)
                acc_ref[branch, rows(loc, tq), :] = pv[:, :HEAD_DIM]
                m_ref[branch, rows(loc, tq), :] = jnp.broadcast_to(m, (tq, LANES))
                d_ref[branch, rows(loc, tq), :] = pv[:, HEAD_DIM:]

    m_all = m_ref[0]
    for branch in range(1, len(plan)):
        m_all = jnp.maximum(m_all, m_ref[branch])
    num = jnp.zeros(o_ref.shape, F32)
    tot = jnp.zeros(o_ref.shape, F32)
    for branch in range(len(plan)):
        w = jnp.exp2(m_ref[branch] - m_all)
        num = num + w * acc_ref[branch]
        tot = tot + w * d_ref[branch]
    o_ref[...] = num / tot


def _mixer_a_plan(seq, window):
    plan = []
    for dil in DILATIONS:
        length = seq // dil
        tq = min(128, length // 2, window // dil)
        assert length >= tq + 2 * HALF_WINDOW and window % (dil * tq) == 0, (seq, window, dil)
        plan.append((dil, tq))
    return tuple(plan)


def _mixer_a(qkv, batch, seq):
    rows = batch * seq
    window = min(2048, seq)
    nwin = seq // window
    plan = _mixer_a_plan(seq, window)
    q_spec = pl.BlockSpec((None, window, HEAD_DIM), lambda b, h, w: (h, b * nwin + w, 0))
    k_spec = pl.BlockSpec((None, seq, HEAD_DIM), lambda b, h, w: (N_HEADS_A + h, b, 0))
    v_spec = pl.BlockSpec((None, seq, HEAD_DIM), lambda b, h, w: (2 * N_HEADS_A + h, b, 0))
    stat = pltpu.VMEM((len(plan), window, LANES), F32)
    return pl.pallas_call(
        functools.partial(_mixer_a_kernel, seq=seq, window=window, plan=plan),
        out_shape=jax.ShapeDtypeStruct((rows, WIDTH_A), F32),
        grid=(batch, N_HEADS_A, nwin),
        in_specs=[q_spec, k_spec, v_spec],
        out_specs=pl.BlockSpec((window, HEAD_DIM), lambda b, h, w: (b * nwin + w, h)),
        scratch_shapes=[stat, stat, stat],
        compiler_params=_params(("parallel", "parallel", "arbitrary")),
        name="mixer_a_dilated",
    )(qkv, qkv, qkv)


SUBLANES = 8
BF16_SUBLANES = 16
MIXER_B_CHAIN_COLS = 256


def _reduce_rows(x, op):
    parts = [x[r:r + SUBLANES] for r in range(0, x.shape[0], SUBLANES)]
    while len(parts) > 1:
        parts = [op(parts[i], parts[i + 1]) for i in range(0, len(parts) - 1, 2)] + (
            [parts[-1]] if len(parts) % 2 else [])
    top = parts[0]
    rows = [top[r:r + 1] for r in range(top.shape[0])]
    while len(rows) > 1:
        rows = [op(rows[i], rows[i + 1]) for i in range(0, len(rows), 2)]
    return rows[0]


def _mixer_b_kernel(q_ref, k_ref, vt_ref, o_ref, m_ref, acc_ref, *, nkv, chains):
    ki = pl.program_id(3)

    @pl.when(ki == 0)
    def _():
        m_ref[...] = jnp.full_like(m_ref, NEG_BIG)
        acc_ref[...] = jnp.zeros_like(acc_ref)

    k = k_ref[...]
    ones = jnp.ones((acc_ref.shape[1] - HEAD_DIM, k.shape[0]), BF16)
    vt = jnp.concatenate([vt_ref[...], ones], axis=0)
    tq = q_ref.shape[1]
    cols = m_ref.shape[-1]
    chains_per_head = tq // cols

    def scores_and_max(c):
        q = q_ref[c // chains_per_head, pl.ds((c % chains_per_head) * cols, cols), :]
        st = lax.dot_general(k, q, (((1,), (1,)), ((), ())), preferred_element_type=F32)
        return st, jnp.maximum(m_ref[c], _reduce_rows(st, jnp.maximum))

    def accumulate(c, st, m_new):
        alpha = jnp.exp2(m_ref[c] - m_new)
        p = jnp.exp2(st - m_new).astype(BF16)
        acc_ref[c] = alpha * acc_ref[c] + jnp.dot(vt, p, preferred_element_type=F32)
        m_ref[c] = m_new

    pending = [scores_and_max(c) for c in range(chains)]
    for c in range(chains):
        accumulate(c, *pending[c])

    @pl.when(ki == nkv - 1)
    def _():
        for c in range(chains):
            head, part = divmod(c, chains_per_head)
            acc = acc_ref[c]
            o_ref[part * cols:(part + 1) * cols, head * HEAD_DIM:(head + 1) * HEAD_DIM] = (
                acc[:HEAD_DIM] / acc[HEAD_DIM:HEAD_DIM + 1]).T


def _mixer_b(q, k, vt, batch, seq):
    rows = batch * seq
    tq = min(1024, seq)
    tkv = min(2048, seq)
    nq, nkv = seq // tq, seq // tkv
    cols = min(MIXER_B_CHAIN_COLS, tq)
    chains = GQA_GROUP * tq // cols
    q_spec = pl.BlockSpec((GQA_GROUP, tq, HEAD_DIM), lambda b, g, qi, ki: (g, b * nq + qi, 0))
    k_spec = pl.BlockSpec((None, tkv, HEAD_DIM), lambda b, g, qi, ki: (g, b * nkv + ki, 0))
    vt_spec = pl.BlockSpec((None, HEAD_DIM, tkv), lambda b, g, qi, ki: (g, 0, b * nkv + ki))
    return pl.pallas_call(
        functools.partial(_mixer_b_kernel, nkv=nkv, chains=chains),
        out_shape=jax.ShapeDtypeStruct((rows, WIDTH_B), F32),
        grid=(batch, N_KV_B, nq, nkv),
        in_specs=[q_spec, k_spec, vt_spec],
        out_specs=pl.BlockSpec((tq, GQA_GROUP * HEAD_DIM), lambda b, g, qi, ki: (b * nq + qi, g)),
        scratch_shapes=[
            pltpu.VMEM((chains, 1, cols), F32),
            pltpu.VMEM((chains, HEAD_DIM + BF16_SUBLANES, cols), F32),
        ],
        compiler_params=_params(("parallel", "parallel", "parallel", "arbitrary")),
        name="mixer_b_gqa",
    )(q, k, vt)


def _rotary_tables(angle_groups):
    seq = angle_groups[0].shape[0]
    cos = np.ones((seq, HEAD_DIM), np.float64)
    sin_lo = np.zeros((seq, HEAD_DIM), np.float64)
    sin_hi = np.zeros((seq, HEAD_DIM), np.float64)
    lane = 0
    for ang in angle_groups:
        half = ang.shape[1]
        cos[:, lane:lane + half] = np.cos(ang)
        cos[:, lane + half:lane + 2 * half] = np.cos(ang)
        sin_lo[:, lane:lane + half] = -np.sin(ang)
        sin_hi[:, lane + half:lane + 2 * half] = np.sin(ang)
        lane += 2 * half
    return tuple(jnp.asarray(t, F32) for t in (cos, sin_lo, sin_hi))


def _inv_freq(rot_dim, theta):
    return theta ** (-np.arange(0, rot_dim, 2, dtype=np.float64) / rot_dim)


def _partial_rope_tables(seq):
    pos = np.arange(seq, dtype=np.float64)
    return _rotary_tables([pos[:, None] * _inv_freq(PARTIAL_ROT_DIM, ROPE_THETA)[None, :]])


def _axial_rope_tables(seq):
    pos = np.arange(seq)
    inv = _inv_freq(HEAD_DIM // 2, AXIAL_THETA)[None, :]
    row = (pos // GRID_W).astype(np.float64)[:, None]
    col = (pos % GRID_W).astype(np.float64)[:, None]
    return _rotary_tables([row * inv, col * inv])


def _run_group(x, mod, weights, depth):
    batch, seq, d = x.shape
    rows = batch * seq
    alpha = (2 * depth) ** 0.25
    tables_a = _partial_rope_tables(seq)
    tables_b = _axial_rope_tables(seq)
    x = x.reshape(rows, d)

    def chunks(l):
        m = mod[l].reshape(batch, 6, 1, d)
        return [m[:, i] for i in range(6)]

    sh1, sc1, g1, sh2, sc2, g2 = chunks(0)
    h = _modulate(x, sc1, sh1, seq)
    for l in range(depth):
        w = weights[l]
        qkv_a = _gemm_proj_a(h, w["w_in"], tables_a, seq)
        q_b = _gemm_proj_bq(h, w["w_in"], tables_b, w["g_q"], seq)
        k_b, vt_b = _gemm_proj_bkv(h, w["w_in"], tables_b, w["g_k"], seq)
        o_a = _mixer_a(qkv_a, batch, seq)
        o_b = _mixer_b(q_b, k_b, vt_b, batch, seq)
        mixed = _norm_mix(o_a, o_b, w["g_out_a"], w["g_out_b"], seq)
        y = _gemm_resid(mixed, w["w_out"], x, g1, seq, alpha)
        x, h = _layernorm_modulate(y, w["ln1_g"], w["ln1_b"], sc2, sh2, seq)
        f = _gemm_relu2(h, w["w_up"])
        y = _gemm_resid(f, w["w_down"], x, g2, seq, alpha)
        if l + 1 < depth:
            sh1, sc1, g1, sh2, sc2, g2 = chunks(l + 1)
            x, h = _layernorm_modulate(y, w["ln2_g"], w["ln2_b"], sc1, sh1, seq)
        else:
            x = _layernorm(y, w["ln2_g"], w["ln2_b"], seq)
    return x.reshape(batch, seq, d)


def kernel(x_prompt, x_sample, c_prompt, c_sample, w_ada, b_ada, w_in, g_q, g_k, g_out_a, g_out_b, w_out,
           ln1_g, ln1_b, w_up, w_down, ln2_g, ln2_b):
    depth = w_ada.shape[0]
    n_prompt, n_sample = c_prompt.shape[0], c_sample.shape[0]
    pad = -(n_prompt + n_sample) % 8
    c_rows = jnp.concatenate([c_prompt, c_sample, jnp.zeros((pad, c_prompt.shape[1]), F32)], axis=0)
    mod = _ada(c_rows, w_ada, b_ada)

    w_in, w_out, w_up, w_down = (w.astype(BF16) for w in (w_in, w_out, w_up, w_down))
    weights = [
        dict(w_in=(w_in, l), w_out=(w_out, l), w_up=(w_up, l), w_down=(w_down, l),
             g_q=g_q[l], g_k=g_k[l], g_out_a=g_out_a[l], g_out_b=g_out_b[l],
             ln1_g=ln1_g[l], ln1_b=ln1_b[l], ln2_g=ln2_g[l], ln2_b=ln2_b[l])
        for l in range(depth)
    ]
    y_prompt = _run_group(x_prompt, mod[:, :n_prompt], weights, depth)
    y_sample = _run_group(x_sample, mod[:, n_prompt:n_prompt + n_sample], weights, depth)
    return (y_prompt, y_sample)
```

```python
import functools

import numpy as np
import jax
import jax.numpy as jnp
from jax import lax
from jax.experimental import pallas as pl
from jax.experimental.pallas import tpu as pltpu

HEAD_DIM = 128
N_HEADS_A = 16
N_HEADS_B = 16
N_KV_B = 4
GQA_GROUP = N_HEADS_B // N_KV_B
WIDTH_A = N_HEADS_A * HEAD_DIM
WIDTH_B = N_HEADS_B * HEAD_DIM
KV_WIDTH_B = N_KV_B * HEAD_DIM
PROJ_A = 3 * WIDTH_A
PROJ_B = WIDTH_B + 2 * KV_WIDTH_B
DILATIONS = (1, 4, 16)
HALF_WINDOW = 64
PARTIAL_ROT_DIM = HEAD_DIM // 4
ROPE_THETA = 500000.0
AXIAL_THETA = 10000.0
GRID_W = 64
LN_EPS = 1e-5
RMS_EPS = 1e-6
NEG_BIG = -1e30
ATTN_SCALE = HEAD_DIM ** -0.5
LOG2_E = 1.4426950408889634

V7X_VMEM_BYTES = 64 * 1024 * 1024
VMEM_LIMIT_BYTES = V7X_VMEM_BYTES - 6 * 1024 * 1024
LANES = 128

F32 = jnp.float32
BF16 = jnp.bfloat16


def _params(semantics):
    return pltpu.CompilerParams(dimension_semantics=semantics, vmem_limit_bytes=VMEM_LIMIT_BYTES)


def _ada_kernel(c_ref, w_ref, b_ref, o_ref):
    c = c_ref[...]
    s = (c * (1.0 / (1.0 + jnp.exp(-c)))).astype(BF16)
    w = w_ref[...].astype(BF16)
    o_ref[...] = jnp.dot(s, w, preferred_element_type=F32) + b_ref[...]


def _ada(c_rows, w_ada, b_ada):
    depth, d, n = w_ada.shape
    rows = c_rows.shape[0]
    tn = min(512, n)
    return pl.pallas_call(
        _ada_kernel,
        out_shape=jax.ShapeDtypeStruct((depth, rows, n), F32),
        grid=(depth, n // tn),
        in_specs=[
            pl.BlockSpec((rows, d), lambda l, j: (0, 0)),
            pl.BlockSpec((None, d, tn), lambda l, j: (l, 0, j)),
            pl.BlockSpec((None, 1, tn), lambda l, j: (l, 0, j)),
        ],
        out_specs=pl.BlockSpec((None, rows, tn), lambda l, j: (l, 0, j)),
        compiler_params=_params(("parallel", "parallel")),
        name="ada_modulation",
    )(c_rows, w_ada, b_ada.reshape(depth, 1, n))


def _modulate_kernel(x_ref, sc_ref, sh_ref, h_ref):
    h_ref[...] = (x_ref[...] * (1.0 + sc_ref[...]) + sh_ref[...]).astype(BF16)


def _layernorm_rows(y, g, b):
    mu = jnp.mean(y, axis=-1, keepdims=True)
    yc = y - mu
    var = jnp.mean(yc * yc, axis=-1, keepdims=True)
    return yc * lax.rsqrt(var + LN_EPS) * g + b


def _ln_kernel(y_ref, g_ref, b_ref, x_ref):
    x_ref[...] = _layernorm_rows(y_ref[...], g_ref[...], b_ref[...])


def _ln_modulate_kernel(y_ref, g_ref, b_ref, sc_ref, sh_ref, x_ref, h_ref):
    x = _layernorm_rows(y_ref[...], g_ref[...], b_ref[...])
    x_ref[...] = x
    h_ref[...] = (x * (1.0 + sc_ref[...]) + sh_ref[...]).astype(BF16)


def _rms_rows(o, g):
    return o * lax.rsqrt(jnp.mean(o * o, axis=-1, keepdims=True) + RMS_EPS) * g


def _norm_mix_kernel(oa_ref, ob_ref, ga_ref, gb_ref, m_ref):
    wa = oa_ref.shape[-1]
    m_ref[:, :wa] = _rms_rows(oa_ref[...], ga_ref[...]).astype(BF16)
    m_ref[:, wa:] = _rms_rows(ob_ref[...], gb_ref[...]).astype(BF16)


def _row_tile(rows, seq):
    return min(256, seq, rows)


def _modulate(x, sc, sh, seq):
    rows, d = x.shape
    tm = _row_tile(rows, seq)
    per_batch = lambda i: ((i * tm) // seq, 0, 0)
    return pl.pallas_call(
        _modulate_kernel,
        out_shape=jax.ShapeDtypeStruct((rows, d), BF16),
        grid=(rows // tm,),
        in_specs=[
            pl.BlockSpec((tm, d), lambda i: (i, 0)),
            pl.BlockSpec((None, 1, d), per_batch),
            pl.BlockSpec((None, 1, d), per_batch),
        ],
        out_specs=pl.BlockSpec((tm, d), lambda i: (i, 0)),
        compiler_params=_params(("parallel",)),
        name="modulate",
    )(x, sc, sh)


def _layernorm(y, g, b, seq):
    rows, d = y.shape
    tm = _row_tile(rows, seq)
    return pl.pallas_call(
        _ln_kernel,
        out_shape=jax.ShapeDtypeStruct((rows, d), F32),
        grid=(rows // tm,),
        in_specs=[
            pl.BlockSpec((tm, d), lambda i: (i, 0)),
            pl.BlockSpec((1, d), lambda i: (0, 0)),
            pl.BlockSpec((1, d), lambda i: (0, 0)),
        ],
        out_specs=pl.BlockSpec((tm, d), lambda i: (i, 0)),
        compiler_params=_params(("parallel",)),
        name="layernorm",
    )(y, g.reshape(1, d), b.reshape(1, d))


def _layernorm_modulate(y, g, b, sc, sh, seq):
    rows, d = y.shape
    tm = _row_tile(rows, seq)
    per_batch = lambda i: ((i * tm) // seq, 0, 0)
    return pl.pallas_call(
        _ln_modulate_kernel,
        out_shape=(jax.ShapeDtypeStruct((rows, d), F32), jax.ShapeDtypeStruct((rows, d), BF16)),
        grid=(rows // tm,),
        in_specs=[
            pl.BlockSpec((tm, d), lambda i: (i, 0)),
            pl.BlockSpec((1, d), lambda i: (0, 0)),
            pl.BlockSpec((1, d), lambda i: (0, 0)),
            pl.BlockSpec((None, 1, d), per_batch),
            pl.BlockSpec((None, 1, d), per_batch),
        ],
        out_specs=(pl.BlockSpec((tm, d), lambda i: (i, 0)), pl.BlockSpec((tm, d), lambda i: (i, 0))),
        compiler_params=_params(("parallel",)),
        name="layernorm_modulate",
    )(y, g.reshape(1, d), b.reshape(1, d), sc, sh)


def _norm_mix(o_a, o_b, g_a, g_b, seq):
    rows, wa = o_a.shape
    wb = o_b.shape[1]
    tm = _row_tile(rows, seq)
    return pl.pallas_call(
        _norm_mix_kernel,
        out_shape=jax.ShapeDtypeStruct((rows, wa + wb), BF16),
        grid=(rows // tm,),
        in_specs=[
            pl.BlockSpec((tm, wa), lambda i: (i, 0)),
            pl.BlockSpec((tm, wb), lambda i: (i, 0)),
            pl.BlockSpec((1, wa), lambda i: (0, 0)),
            pl.BlockSpec((1, wb), lambda i: (0, 0)),
        ],
        out_specs=pl.BlockSpec((tm, wa + wb), lambda i: (i, 0)),
        compiler_params=_params(("parallel",)),
        name="norm_mix",
    )(o_a, o_b, g_a.reshape(1, wa), g_b.reshape(1, wb))


def _accumulate(a_ref, w_ref, acc_ref, nk, epilogue):
    def part():
        return jnp.dot(a_ref[...], w_ref[...], preferred_element_type=F32)

    if nk == 1:
        epilogue(part())
        return
    k = pl.program_id(2)

    @pl.when(k == 0)
    def _():
        acc_ref[...] = part()

    @pl.when(jnp.logical_and(k > 0, k < nk - 1))
    def _():
        acc_ref[...] += part()

    @pl.when(k == nk - 1)
    def _():
        epilogue(acc_ref[...] + part())


def _rotate_pairs(y, cos, sin):
    return y * cos + pltpu.roll(y, HEAD_DIM // 2, 1) * sin


def _gemm_relu2_kernel(a_ref, w_ref, o_ref, *scratch, nk):
    def epilogue(acc):
        o_ref[...] = jnp.square(jnp.maximum(acc, 0.0)).astype(o_ref.dtype)
    _accumulate(a_ref, w_ref, scratch[0] if scratch else None, nk, epilogue)


def _gemm_resid_kernel(a_ref, w_ref, x_ref, gate_ref, o_ref, *scratch, nk, alpha):
    def epilogue(acc):
        o_ref[...] = alpha * x_ref[...] + gate_ref[...] * acc
    _accumulate(a_ref, w_ref, scratch[0] if scratch else None, nk, epilogue)


def _gemm_proj_a_kernel(a_ref, w_ref, cos_ref, sin_ref, o_ref, *scratch, nk, rope_tiles):
    heads = o_ref.shape[0]

    def epilogue(acc):
        j = pl.program_id(1)

        @pl.when(j < rope_tiles)
        def _():
            for hh in range(heads):
                y = acc[:, hh * HEAD_DIM:(hh + 1) * HEAD_DIM]
                o_ref[hh] = _rotate_pairs(y, cos_ref[...], sin_ref[...])

        @pl.when(j >= rope_tiles)
        def _():
            for hh in range(heads):
                o_ref[hh] = acc[:, hh * HEAD_DIM:(hh + 1) * HEAD_DIM]

    _accumulate(a_ref, w_ref, scratch[0] if scratch else None, nk, epilogue)


def _axial_normed_rotated(y, g_ref, cos_ref, sin_ref):
    return _rotate_pairs(_rms_rows(y, g_ref[...]), cos_ref[...], sin_ref[...])


def _gemm_proj_bq_kernel(a_ref, w_ref, cos_ref, sin_ref, gq_ref, o_ref, *scratch, nk):
    def epilogue(acc):
        for hh in range(o_ref.shape[0]):
            y = acc[:, hh * HEAD_DIM:(hh + 1) * HEAD_DIM]
            q = _axial_normed_rotated(y, gq_ref, cos_ref, sin_ref)
            o_ref[hh] = (q * (ATTN_SCALE * LOG2_E)).astype(o_ref.dtype)

    _accumulate(a_ref, w_ref, scratch[0] if scratch else None, nk, epilogue)


def _gemm_proj_bkv_kernel(a_ref, w_ref, cos_ref, sin_ref, gk_ref, k_ref, vt_ref, *scratch, nk):
    def epilogue(acc):
        for hh in range(N_KV_B):
            y = acc[:, hh * HEAD_DIM:(hh + 1) * HEAD_DIM]
            k_ref[hh] = _axial_normed_rotated(y, gk_ref, cos_ref, sin_ref).astype(k_ref.dtype)
        for hh in range(N_KV_B):
            y = acc[:, (N_KV_B + hh) * HEAD_DIM:(N_KV_B + hh + 1) * HEAD_DIM]
            vt_ref[hh] = y.T.astype(vt_ref.dtype)

    _accumulate(a_ref, w_ref, scratch[0] if scratch else None, nk, epilogue)


def _gemm_tiles(m, n, k, seq):
    tm = min(1024, seq, m)
    tn = min(1024, n)
    tk = k if k <= 4096 else 2048
    return tm, tn, tk


def _gemm_call(kernel_fn, a, w, extra_inputs, extra_specs, out_shape, out_spec, tiles, n_cols, col_offset, name):
    w_all, layer = w
    m, k = a.shape
    tm, tn, tk = tiles
    nk = k // tk
    joff = col_offset // tn
    scratch = [pltpu.VMEM((tm, tn), F32)] if nk > 1 else []
    return pl.pallas_call(
        functools.partial(kernel_fn, nk=nk),
        out_shape=out_shape,
        grid=(m // tm, n_cols // tn, nk),
        in_specs=[
            pl.BlockSpec((tm, tk), lambda i, j, kk: (i, kk)),
            pl.BlockSpec((None, tk, tn), lambda i, j, kk: (layer, kk, j + joff)),
        ] + extra_specs,
        out_specs=out_spec,
        scratch_shapes=scratch,
        compiler_params=_params(("parallel", "parallel", "arbitrary")),
        name=name,
    )(a, w_all, *extra_inputs)


def _gemm_relu2(a, w):
    m, k = a.shape
    n = w[0].shape[2]
    tiles = _gemm_tiles(m, n, k, m)
    tm, tn, _ = tiles
    return _gemm_call(
        _gemm_relu2_kernel, a, w, [], [],
        jax.ShapeDtypeStruct((m, n), BF16),
        pl.BlockSpec((tm, tn), lambda i, j, kk: (i, j)),
        tiles, n, 0, "gemm_relu2")


def _gemm_resid(a, w, x, gate, seq, alpha):
    m, k = a.shape
    n = w[0].shape[2]
    tiles = _gemm_tiles(m, n, k, seq)
    tm, tn, _ = tiles
    return _gemm_call(
        functools.partial(_gemm_resid_kernel, alpha=alpha), a, w, [x, gate],
        [pl.BlockSpec((tm, tn), lambda i, j, kk: (i, j)),
         pl.BlockSpec((None, 1, tn), lambda i, j, kk: ((i * tm) // seq, 0, j))],
        jax.ShapeDtypeStruct((m, n), F32),
        pl.BlockSpec((tm, tn), lambda i, j, kk: (i, j)),
        tiles, n, 0, "gemm_resid")


def _rope_specs(tm, seq):
    nrow = seq // tm
    return [pl.BlockSpec((tm, LANES), lambda i, j, kk: (i % nrow, 0))] * 2


def _gemm_proj_a(h, w_in, tables, seq):
    m, k = h.shape
    tiles = _gemm_tiles(m, PROJ_A, k, seq)
    tm, tn, _ = tiles
    heads = tn // HEAD_DIM
    return _gemm_call(
        functools.partial(_gemm_proj_a_kernel, rope_tiles=2 * WIDTH_A // tn), h, w_in, list(tables),
        _rope_specs(tm, seq),
        jax.ShapeDtypeStruct((PROJ_A // HEAD_DIM, m, HEAD_DIM), F32),
        pl.BlockSpec((heads, tm, HEAD_DIM), lambda i, j, kk: (j, i, 0)),
        tiles, PROJ_A, 0, "gemm_proj_a")


def _gemm_proj_bq(h, w_in, tables, g_q, seq):
    m, k = h.shape
    tiles = _gemm_tiles(m, WIDTH_B, k, seq)
    tm, tn, _ = tiles
    heads = tn // HEAD_DIM
    return _gemm_call(
        _gemm_proj_bq_kernel, h, w_in, list(tables) + [g_q.reshape(1, HEAD_DIM)],
        _rope_specs(tm, seq) + [pl.BlockSpec((1, HEAD_DIM), lambda i, j, kk: (0, 0))],
        jax.ShapeDtypeStruct((N_HEADS_B, m, HEAD_DIM), BF16),
        pl.BlockSpec((heads, tm, HEAD_DIM), lambda i, j, kk: (j, i, 0)),
        tiles, WIDTH_B, PROJ_A, "gemm_proj_bq")


def _gemm_proj_bkv(h, w_in, tables, g_k, seq):
    m, k = h.shape
    tm, _, tk = _gemm_tiles(m, 2 * KV_WIDTH_B, k, seq)
    tiles = (tm, 2 * KV_WIDTH_B, tk)
    return _gemm_call(
        _gemm_proj_bkv_kernel, h, w_in, list(tables) + [g_k.reshape(1, HEAD_DIM)],
        _rope_specs(tm, seq) + [pl.BlockSpec((1, HEAD_DIM), lambda i, j, kk: (0, 0))],
        (jax.ShapeDtypeStruct((N_KV_B, m, HEAD_DIM), BF16), jax.ShapeDtypeStruct((N_KV_B, HEAD_DIM, m), BF16)),
        (pl.BlockSpec((N_KV_B, tm, HEAD_DIM), lambda i, j, kk: (0, i, 0)),
         pl.BlockSpec((N_KV_B, HEAD_DIM, tm), lambda i, j, kk: (0, 0, i))),
        tiles, 2 * KV_WIDTH_B, PROJ_A + WIDTH_B, "gemm_proj_bkv")


def _mixer_a_kernel(q_ref, k_ref, v_ref, o_ref, acc_ref, m_ref, d_ref, *, seq, window, plan):
    win = pl.program_id(2)

    for branch, (dil, tq) in enumerate(plan):
        length = seq // dil
        tkv = tq + 2 * HALF_WINDOW
        blocks = window // (dil * tq)
        offset = (lax.broadcasted_iota(jnp.int32, (tq, tkv), 1)
                  - lax.broadcasted_iota(jnp.int32, (tq, tkv), 0))

        def rows(start, size, dil=dil):
            return pl.ds(start, size) if dil == 1 else pl.ds(start, size, stride=dil)

        for res in range(dil):
            for blk in range(blocks):
                loc = res + dil * tq * blk
                q0 = win * (window // dil) + tq * blk
                k0 = jnp.clip(q0 - HALF_WINDOW, 0, length - tkv)
                q = (q_ref[rows(loc, tq), :] * (ATTN_SCALE * LOG2_E)).astype(BF16)
                k = k_ref[rows(res + dil * k0, tkv), :].astype(BF16)
                v = v_ref[rows(res + dil * k0, tkv), :].astype(BF16)
                s = lax.dot_general(q, k, (((1,), (1,)), ((), ())), preferred_element_type=F32)
                shifted = offset + (k0 - q0 + HALF_WINDOW)
                valid = lax.bitcast_convert_type(shifted, jnp.uint32) <= 2 * HALF_WINDOW
                s = jnp.where(valid, s, NEG_BIG)
                m = jnp.max(s, axis=-1, keepdims=True)
                p = jnp.exp2(s - m).astype(BF16)
                v_ones = jnp.concatenate([v, jnp.ones((tkv, LANES), BF16)], axis=1)
                pv = jnp.dot(p, v_ones, preferred_element_type=F32)
                acc_ref[branch, rows(loc, tq), :] = pv[:, :HEAD_DIM]
                m_ref[branch, rows(loc, tq), :] = jnp.broadcast_to(m, (tq, LANES))
                d_ref[branch, rows(loc, tq), :] = pv[:, HEAD_DIM:]

    m_all = m_ref[0]
    for branch in range(1, len(plan)):
        m_all = jnp.maximum(m_all, m_ref[branch])
    num = jnp.zeros(o_ref.shape, F32)
    tot = jnp.zeros(o_ref.shape, F32)
    for branch in range(len(plan)):
        w = jnp.exp2(m_ref[branch] - m_all)
        num = num + w * acc_ref[branch]
        tot = tot + w * d_ref[branch]
    o_ref[...] = num / tot


def _mixer_a_plan(seq, window):
    plan = []
    for dil in DILATIONS:
        length = seq // dil
        tq = min(128, length // 2, window // dil)
        assert length >= tq + 2 * HALF_WINDOW and window % (dil * tq) == 0, (seq, window, dil)
        plan.append((dil, tq))
    return tuple(plan)


def _mixer_a(qkv, batch, seq):
    rows = batch * seq
    window = min(2048, seq)
    nwin = seq // window
    plan = _mixer_a_plan(seq, window)
    q_spec = pl.BlockSpec((None, window, HEAD_DIM), lambda b, h, w: (h, b * nwin + w, 0))
    k_spec = pl.BlockSpec((None, seq, HEAD_DIM), lambda b, h, w: (N_HEADS_A + h, b, 0))
    v_spec = pl.BlockSpec((None, seq, HEAD_DIM), lambda b, h, w: (2 * N_HEADS_A + h, b, 0))
    stat = pltpu.VMEM((len(plan), window, LANES), F32)
    return pl.pallas_call(
        functools.partial(_mixer_a_kernel, seq=seq, window=window, plan=plan),
        out_shape=jax.ShapeDtypeStruct((rows, WIDTH_A), F32),
        grid=(batch, N_HEADS_A, nwin),
        in_specs=[q_spec, k_spec, v_spec],
        out_specs=pl.BlockSpec((window, HEAD_DIM), lambda b, h, w: (b * nwin + w, h)),
        scratch_shapes=[stat, stat, stat],
        compiler_params=_params(("parallel", "parallel", "arbitrary")),
        name="mixer_a_dilated",
    )(qkv, qkv, qkv)


SUBLANES = 8
BF16_SUBLANES = 16
MIXER_B_CHAIN_COLS = 256


def _reduce_rows(x, op):
    parts = [x[r:r + SUBLANES] for r in range(0, x.shape[0], SUBLANES)]
    while len(parts) > 1:
        parts = [op(parts[i], parts[i + 1]) for i in range(0, len(parts) - 1, 2)] + (
            [parts[-1]] if len(parts) % 2 else [])
    top = parts[0]
    rows = [top[r:r + 1] for r in range(top.shape[0])]
    while len(rows) > 1:
        rows = [op(rows[i], rows[i + 1]) for i in range(0, len(rows), 2)]
    return rows[0]


def _mixer_b_kernel(q_ref, k_ref, vt_ref, o_ref, m_ref, acc_ref, *, nkv, chains):
    ki = pl.program_id(3)

    @pl.when(ki == 0)
    def _():
        m_ref[...] = jnp.full_like(m_ref, NEG_BIG)
        acc_ref[...] = jnp.zeros_like(acc_ref)

    k = k_ref[...]
    ones = jnp.ones((acc_ref.shape[1] - HEAD_DIM, k.shape[0]), BF16)
    vt = jnp.concatenate([vt_ref[...], ones], axis=0)
    tq = q_ref.shape[1]
    cols = m_ref.shape[-1]
    chains_per_head = tq // cols

    def scores_and_max(c):
        q = q_ref[c // chains_per_head, pl.ds((c % chains_per_head) * cols, cols), :]
        st = lax.dot_general(k, q, (((1,), (1,)), ((), ())), preferred_element_type=F32)
        return st, jnp.maximum(m_ref[c], _reduce_rows(st, jnp.maximum))

    def accumulate(c, st, m_new):
        alpha = jnp.exp2(m_ref[c] - m_new)
        p = jnp.exp2(st - m_new).astype(BF16)
        acc_ref[c] = alpha * acc_ref[c] + jnp.dot(vt, p, preferred_element_type=F32)
        m_ref[c] = m_new

    pending = [scores_and_max(c) for c in range(chains)]
    for c in range(chains):
        accumulate(c, *pending[c])

    @pl.when(ki == nkv - 1)
    def _():
        for c in range(chains):
            head, part = divmod(c, chains_per_head)
            acc = acc_ref[c]
            o_ref[part * cols:(part + 1) * cols, head * HEAD_DIM:(head + 1) * HEAD_DIM] = (
                acc[:HEAD_DIM] / acc[HEAD_DIM:HEAD_DIM + 1]).T


def _mixer_b(q, k, vt, batch, seq):
    rows = batch * seq
    tq = min(1024, seq)
    tkv = min(2048, seq)
    nq, nkv = seq // tq, seq // tkv
    cols = min(MIXER_B_CHAIN_COLS, tq)
    chains = GQA_GROUP * tq // cols
    q_spec = pl.BlockSpec((GQA_GROUP, tq, HEAD_DIM), lambda b, g, qi, ki: (g, b * nq + qi, 0))
    k_spec = pl.BlockSpec((None, tkv, HEAD_DIM), lambda b, g, qi, ki: (g, b * nkv + ki, 0))
    vt_spec = pl.BlockSpec((None, HEAD_DIM, tkv), lambda b, g, qi, ki: (g, 0, b * nkv + ki))
    return pl.pallas_call(
        functools.partial(_mixer_b_kernel, nkv=nkv, chains=chains),
        out_shape=jax.ShapeDtypeStruct((rows, WIDTH_B), F32),
        grid=(batch, N_KV_B, nq, nkv),
        in_specs=[q_spec, k_spec, vt_spec],
        out_specs=pl.BlockSpec((tq, GQA_GROUP * HEAD_DIM), lambda b, g, qi, ki: (b * nq + qi, g)),
        scratch_shapes=[
            pltpu.VMEM((chains, 1, cols), F32),
            pltpu.VMEM((chains, HEAD_DIM + BF16_SUBLANES, cols), F32),
        ],
        compiler_params=_params(("parallel", "parallel", "parallel", "arbitrary")),
        name="mixer_b_gqa",
    )(q, k, vt)


def _paired_layout(rotary_halves):
    lower = [d for a, _, w in rotary_halves for d in range(a, a + w)]
    upper = [d for _, b, w in rotary_halves for d in range(b, b + w)]
    rest = [d for d in range(HEAD_DIM) if d not in lower and d not in upper]
    lower += rest[:len(rest) // 2]
    upper += rest[len(rest) // 2:]
    return np.array(lower + upper)


PERM_A = _paired_layout([(0, PARTIAL_ROT_DIM // 2, PARTIAL_ROT_DIM // 2)])
PERM_B = _paired_layout([(0, HEAD_DIM // 4, HEAD_DIM // 4), (HEAD_DIM // 2, 3 * HEAD_DIM // 4, HEAD_DIM // 4)])


def _rotary_tables(angle_groups):
    seq = angle_groups[0].shape[0]
    cos = np.ones((seq, HEAD_DIM), np.float64)
    sin = np.zeros((seq, HEAD_DIM), np.float64)
    lane = 0
    for ang in angle_groups:
        w = ang.shape[1]
        for base, sign in ((lane, -1.0), (lane + HEAD_DIM // 2, 1.0)):
            cos[:, base:base + w] = np.cos(ang)
            sin[:, base:base + w] = sign * np.sin(ang)
        lane += w
    return tuple(jnp.asarray(t, F32) for t in (cos, sin))


def _inv_freq(rot_dim, theta):
    return theta ** (-np.arange(0, rot_dim, 2, dtype=np.float64) / rot_dim)


def _partial_rope_tables(seq):
    pos = np.arange(seq, dtype=np.float64)
    return _rotary_tables([pos[:, None] * _inv_freq(PARTIAL_ROT_DIM, ROPE_THETA)[None, :]])


def _axial_rope_tables(seq):
    pos = np.arange(seq)
    inv = _inv_freq(HEAD_DIM // 2, AXIAL_THETA)[None, :]
    row = (pos // GRID_W).astype(np.float64)[:, None]
    col = (pos % GRID_W).astype(np.float64)[:, None]
    return _rotary_tables([row * inv, col * inv])


def _run_group(x, mod, weights, depth):
    batch, seq, d = x.shape
    rows = batch * seq
    alpha = (2 * depth) ** 0.25
    tables_a = _partial_rope_tables(seq)
    tables_b = _axial_rope_tables(seq)
    x = x.reshape(rows, d)

    def chunks(l):
        m = mod[l].reshape(batch, 6, 1, d)
        return [m[:, i] for i in range(6)]

    sh1, sc1, g1, sh2, sc2, g2 = chunks(0)
    h = _modulate(x, sc1, sh1, seq)
    for l in range(depth):
        w = weights[l]
        qkv_a = _gemm_proj_a(h, w["w_in"], tables_a, seq)
        q_b = _gemm_proj_bq(h, w["w_in"], tables_b, w["g_q"], seq)
        k_b, vt_b = _gemm_proj_bkv(h, w["w_in"], tables_b, w["g_k"], seq)
        o_a = _mixer_a(qkv_a, batch, seq)
        o_b = _mixer_b(q_b, k_b, vt_b, batch, seq)
        mixed = _norm_mix(o_a, o_b, w["g_out_a"], w["g_out_b"], seq)
        y = _gemm_resid(mixed, w["w_out"], x, g1, seq, alpha)
        x, h = _layernorm_modulate(y, w["ln1_g"], w["ln1_b"], sc2, sh2, seq)
        f = _gemm_relu2(h, w["w_up"])
        y = _gemm_resid(f, w["w_down"], x, g2, seq, alpha)
        if l + 1 < depth:
            sh1, sc1, g1, sh2, sc2, g2 = chunks(l + 1)
            x, h = _layernorm_modulate(y, w["ln2_g"], w["ln2_b"], sc1, sh1, seq)
        else:
            x = _layernorm(y, w["ln2_g"], w["ln2_b"], seq)
    return x.reshape(batch, seq, d)


def _permute_head_dims(w_in):
    lead = w_in.shape[:-1]

    def heads(lo, hi, perm):
        block = w_in[..., lo:hi].reshape(*lead, (hi - lo) // HEAD_DIM, HEAD_DIM)
        runs = np.split(perm, np.where(np.diff(perm) != 1)[0] + 1)
        block = jnp.concatenate([block[..., r[0]:r[-1] + 1] for r in runs], axis=-1)
        return block.reshape(*lead, hi - lo)

    qk_a_end = 2 * WIDTH_A
    qk_b_end = PROJ_A + WIDTH_B + KV_WIDTH_B
    return jnp.concatenate([
        heads(0, qk_a_end, PERM_A), w_in[..., qk_a_end:PROJ_A],
        heads(PROJ_A, qk_b_end, PERM_B), w_in[..., qk_b_end:]], axis=-1)


def kernel(x_prompt, x_sample, c_prompt, c_sample, w_ada, b_ada, w_in, g_q, g_k, g_out_a, g_out_b, w_out,
           ln1_g, ln1_b, w_up, w_down, ln2_g, ln2_b):
    depth = w_ada.shape[0]
    n_prompt, n_sample = c_prompt.shape[0], c_sample.shape[0]
    pad = -(n_prompt + n_sample) % 8
    c_rows = jnp.concatenate([c_prompt, c_sample, jnp.zeros((pad, c_prompt.shape[1]), F32)], axis=0)
    mod = _ada(c_rows, w_ada, b_ada)

    w_in = _permute_head_dims(w_in)
    g_q, g_k = g_q[:, PERM_B], g_k[:, PERM_B]
    w_in, w_out, w_up, w_down = (w.astype(BF16) for w in (w_in, w_out, w_up, w_down))
    weights = [
        dict(w_in=(w_in, l), w_out=(w_out, l), w_up=(w_up, l), w_down=(w_down, l),
             g_q=g_q[l], g_k=g_k[l], g_out_a=g_out_a[l], g_out_b=g_out_b[l],
             ln1_g=ln1_g[l], ln1_b=ln1_b[l], ln2_g=ln2_g[l], ln2_b=ln2_b[l])
        for l in range(depth)
    ]
    y_prompt = _run_group(x_prompt, mod[:, :n_prompt], weights, depth)
    y_sample = _run_group(x_sample, mod[:, n_prompt:n_prompt + n_sample], weights, depth)
    return (y_prompt, y_sample)
```

```python
import functools

import numpy as np
import jax
import jax.numpy as jnp
from jax import lax
from jax.experimental import pallas as pl
from jax.experimental.pallas import tpu as pltpu

HEAD_DIM = 128
N_HEADS_A = 16
N_HEADS_B = 16
N_KV_B = 4
GQA_GROUP = N_HEADS_B // N_KV_B
WIDTH_A = N_HEADS_A * HEAD_DIM
WIDTH_B = N_HEADS_B * HEAD_DIM
KV_WIDTH_B = N_KV_B * HEAD_DIM
PROJ_A = 3 * WIDTH_A
PROJ_B = WIDTH_B + 2 * KV_WIDTH_B
DILATIONS = (1, 4, 16)
HALF_WINDOW = 64
PARTIAL_ROT_DIM = HEAD_DIM // 4
ROPE_THETA = 500000.0
AXIAL_THETA = 10000.0
GRID_W = 64
LN_EPS = 1e-5
RMS_EPS = 1e-6
NEG_BIG = -1e30
ATTN_SCALE = HEAD_DIM ** -0.5
LOG2_E = 1.4426950408889634

V7X_VMEM_BYTES = 64 * 1024 * 1024
VMEM_LIMIT_BYTES = V7X_VMEM_BYTES - 6 * 1024 * 1024
LANES = 128

F32 = jnp.float32
BF16 = jnp.bfloat16


def _params(semantics):
    return pltpu.CompilerParams(dimension_semantics=semantics, vmem_limit_bytes=VMEM_LIMIT_BYTES)


def _ada_kernel(c_ref, w_ref, b_ref, o_ref):
    c = c_ref[...]
    s = (c * (1.0 / (1.0 + jnp.exp(-c)))).astype(BF16)
    w = w_ref[...].astype(BF16)
    o_ref[...] = jnp.dot(s, w, preferred_element_type=F32) + b_ref[...]


def _ada(c_rows, w_ada, b_ada):
    depth, d, n = w_ada.shape
    rows = c_rows.shape[0]
    tn = min(512, n)
    return pl.pallas_call(
        _ada_kernel,
        out_shape=jax.ShapeDtypeStruct((depth, rows, n), F32),
        grid=(depth, n // tn),
        in_specs=[
            pl.BlockSpec((rows, d), lambda l, j: (0, 0)),
            pl.BlockSpec((None, d, tn), lambda l, j: (l, 0, j)),
            pl.BlockSpec((None, 1, tn), lambda l, j: (l, 0, j)),
        ],
        out_specs=pl.BlockSpec((None, rows, tn), lambda l, j: (l, 0, j)),
        compiler_params=_params(("parallel", "parallel")),
        name="ada_modulation",
    )(c_rows, w_ada, b_ada.reshape(depth, 1, n))


def _modulate_kernel(x_ref, sc_ref, sh_ref, h_ref):
    h_ref[...] = (x_ref[...] * (1.0 + sc_ref[...]) + sh_ref[...]).astype(BF16)


def _layernorm_rows(y, g, b):
    mu = jnp.mean(y, axis=-1, keepdims=True)
    yc = y - mu
    var = jnp.mean(yc * yc, axis=-1, keepdims=True)
    return yc * lax.rsqrt(var + LN_EPS) * g + b


def _ln_kernel(y_ref, g_ref, b_ref, x_ref):
    x_ref[...] = _layernorm_rows(y_ref[...], g_ref[...], b_ref[...])


def _ln_modulate_kernel(y_ref, g_ref, b_ref, sc_ref, sh_ref, x_ref, h_ref):
    x = _layernorm_rows(y_ref[...], g_ref[...], b_ref[...])
    x_ref[...] = x
    h_ref[...] = (x * (1.0 + sc_ref[...]) + sh_ref[...]).astype(BF16)


def _rms_rows(o, g):
    return o * lax.rsqrt(jnp.mean(o * o, axis=-1, keepdims=True) + RMS_EPS) * g


def _norm_mix_kernel(oa_ref, ob_ref, ga_ref, gb_ref, m_ref):
    wa = oa_ref.shape[-1]
    m_ref[:, :wa] = _rms_rows(oa_ref[...], ga_ref[...]).astype(BF16)
    m_ref[:, wa:] = _rms_rows(ob_ref[...], gb_ref[...]).astype(BF16)


def _row_tile(rows, seq):
    return min(256, seq, rows)


def _modulate(x, sc, sh, seq):
    rows, d = x.shape
    tm = _row_tile(rows, seq)
    per_batch = lambda i: ((i * tm) // seq, 0, 0)
    return pl.pallas_call(
        _modulate_kernel,
        out_shape=jax.ShapeDtypeStruct((rows, d), BF16),
        grid=(rows // tm,),
        in_specs=[
            pl.BlockSpec((tm, d), lambda i: (i, 0)),
            pl.BlockSpec((None, 1, d), per_batch),
            pl.BlockSpec((None, 1, d), per_batch),
        ],
        out_specs=pl.BlockSpec((tm, d), lambda i: (i, 0)),
        compiler_params=_params(("parallel",)),
        name="modulate",
    )(x, sc, sh)


def _layernorm(y, g, b, seq):
    rows, d = y.shape
    tm = _row_tile(rows, seq)
    return pl.pallas_call(
        _ln_kernel,
        out_shape=jax.ShapeDtypeStruct((rows, d), F32),
        grid=(rows // tm,),
        in_specs=[
            pl.BlockSpec((tm, d), lambda i: (i, 0)),
            pl.BlockSpec((1, d), lambda i: (0, 0)),
            pl.BlockSpec((1, d), lambda i: (0, 0)),
        ],
        out_specs=pl.BlockSpec((tm, d), lambda i: (i, 0)),
        compiler_params=_params(("parallel",)),
        name="layernorm",
    )(y, g.reshape(1, d), b.reshape(1, d))


def _layernorm_modulate(y, g, b, sc, sh, seq):
    rows, d = y.shape
    tm = _row_tile(rows, seq)
    per_batch = lambda i: ((i * tm) // seq, 0, 0)
    return pl.pallas_call(
        _ln_modulate_kernel,
        out_shape=(jax.ShapeDtypeStruct((rows, d), F32), jax.ShapeDtypeStruct((rows, d), BF16)),
        grid=(rows // tm,),
        in_specs=[
            pl.BlockSpec((tm, d), lambda i: (i, 0)),
            pl.BlockSpec((1, d), lambda i: (0, 0)),
            pl.BlockSpec((1, d), lambda i: (0, 0)),
            pl.BlockSpec((None, 1, d), per_batch),
            pl.BlockSpec((None, 1, d), per_batch),
        ],
        out_specs=(pl.BlockSpec((tm, d), lambda i: (i, 0)), pl.BlockSpec((tm, d), lambda i: (i, 0))),
        compiler_params=_params(("parallel",)),
        name="layernorm_modulate",
    )(y, g.reshape(1, d), b.reshape(1, d), sc, sh)


def _norm_mix(o_a, o_b, g_a, g_b, seq):
    rows, wa = o_a.shape
    wb = o_b.shape[1]
    tm = _row_tile(rows, seq)
    return pl.pallas_call(
        _norm_mix_kernel,
        out_shape=jax.ShapeDtypeStruct((rows, wa + wb), BF16),
        grid=(rows // tm,),
        in_specs=[
            pl.BlockSpec((tm, wa), lambda i: (i, 0)),
            pl.BlockSpec((tm, wb), lambda i: (i, 0)),
            pl.BlockSpec((1, wa), lambda i: (0, 0)),
            pl.BlockSpec((1, wb), lambda i: (0, 0)),
        ],
        out_specs=pl.BlockSpec((tm, wa + wb), lambda i: (i, 0)),
        compiler_params=_params(("parallel",)),
        name="norm_mix",
    )(o_a, o_b, g_a.reshape(1, wa), g_b.reshape(1, wb))


def _accumulate(a_ref, w_ref, acc_ref, nk, epilogue):
    def part():
        return jnp.dot(a_ref[...], w_ref[...], preferred_element_type=F32)

    if nk == 1:
        epilogue(part())
        return
    k = pl.program_id(2)

    @pl.when(k == 0)
    def _():
        acc_ref[...] = part()

    @pl.when(jnp.logical_and(k > 0, k < nk - 1))
    def _():
        acc_ref[...] += part()

    @pl.when(k == nk - 1)
    def _():
        epilogue(acc_ref[...] + part())


def _rotate_pairs(y, cos, sin):
    return y * cos + pltpu.roll(y, HEAD_DIM // 2, 1) * sin


def _gemm_relu2_kernel(a_ref, w_ref, o_ref, *scratch, nk):
    def epilogue(acc):
        o_ref[...] = jnp.square(jnp.maximum(acc, 0.0)).astype(o_ref.dtype)
    _accumulate(a_ref, w_ref, scratch[0] if scratch else None, nk, epilogue)


def _gemm_resid_kernel(a_ref, w_ref, x_ref, gate_ref, o_ref, *scratch, nk, alpha):
    def epilogue(acc):
        o_ref[...] = alpha * x_ref[...] + gate_ref[...] * acc
    _accumulate(a_ref, w_ref, scratch[0] if scratch else None, nk, epilogue)


def _gemm_proj_a_kernel(a_ref, w_ref, cos_ref, sin_ref, o_ref, *scratch, nk, rope_tiles):
    heads = o_ref.shape[0]

    def epilogue(acc):
        j = pl.program_id(1)

        @pl.when(j < rope_tiles)
        def _():
            for hh in range(heads):
                y = acc[:, hh * HEAD_DIM:(hh + 1) * HEAD_DIM]
                o_ref[hh] = _rotate_pairs(y, cos_ref[...], sin_ref[...])

        @pl.when(j >= rope_tiles)
        def _():
            for hh in range(heads):
                o_ref[hh] = acc[:, hh * HEAD_DIM:(hh + 1) * HEAD_DIM]

    _accumulate(a_ref, w_ref, scratch[0] if scratch else None, nk, epilogue)


def _axial_normed_rotated(y, g_ref, cos_ref, sin_ref):
    return _rotate_pairs(_rms_rows(y, g_ref[...]), cos_ref[...], sin_ref[...])


def _gemm_proj_bq_kernel(a_ref, w_ref, cos_ref, sin_ref, gq_ref, o_ref, *scratch, nk):
    def epilogue(acc):
        for hh in range(o_ref.shape[0]):
            y = acc[:, hh * HEAD_DIM:(hh + 1) * HEAD_DIM]
            q = _axial_normed_rotated(y, gq_ref, cos_ref, sin_ref)
            o_ref[hh] = (q * (ATTN_SCALE * LOG2_E)).astype(o_ref.dtype)

    _accumulate(a_ref, w_ref, scratch[0] if scratch else None, nk, epilogue)


def _gemm_proj_bkv_kernel(a_ref, w_ref, cos_ref, sin_ref, gk_ref, k_ref, vt_ref, *scratch, nk):
    def epilogue(acc):
        for hh in range(N_KV_B):
            y = acc[:, hh * HEAD_DIM:(hh + 1) * HEAD_DIM]
            k_ref[hh] = _axial_normed_rotated(y, gk_ref, cos_ref, sin_ref).astype(k_ref.dtype)
        for hh in range(N_KV_B):
            y = acc[:, (N_KV_B + hh) * HEAD_DIM:(N_KV_B + hh + 1) * HEAD_DIM]
            vt_ref[hh] = y.T.astype(vt_ref.dtype)

    _accumulate(a_ref, w_ref, scratch[0] if scratch else None, nk, epilogue)


def _gemm_tiles(m, n, k, seq):
    tm = min(1024, seq, m)
    tn = min(1024, n)
    tk = k if k <= 4096 else 2048
    return tm, tn, tk


def _gemm_call(kernel_fn, a, w, extra_inputs, extra_specs, out_shape, out_spec, tiles, n_cols, col_offset, name):
    w_all, layer = w
    m, k = a.shape
    tm, tn, tk = tiles
    nk = k // tk
    joff = col_offset // tn
    scratch = [pltpu.VMEM((tm, tn), F32)] if nk > 1 else []
    return pl.pallas_call(
        functools.partial(kernel_fn, nk=nk),
        out_shape=out_shape,
        grid=(m // tm, n_cols // tn, nk),
        in_specs=[
            pl.BlockSpec((tm, tk), lambda i, j, kk: (i, kk)),
            pl.BlockSpec((None, tk, tn), lambda i, j, kk: (layer, kk, j + joff)),
        ] + extra_specs,
        out_specs=out_spec,
        scratch_shapes=scratch,
        compiler_params=_params(("parallel", "parallel", "arbitrary")),
        name=name,
    )(a, w_all, *extra_inputs)


def _gemm_relu2(a, w):
    m, k = a.shape
    n = w[0].shape[2]
    tiles = _gemm_tiles(m, n, k, m)
    tm, tn, _ = tiles
    return _gemm_call(
        _gemm_relu2_kernel, a, w, [], [],
        jax.ShapeDtypeStruct((m, n), BF16),
        pl.BlockSpec((tm, tn), lambda i, j, kk: (i, j)),
        tiles, n, 0, "gemm_relu2")


def _gemm_resid(a, w, x, gate, seq, alpha):
    m, k = a.shape
    n = w[0].shape[2]
    tiles = _gemm_tiles(m, n, k, seq)
    tm, tn, _ = tiles
    return _gemm_call(
        functools.partial(_gemm_resid_kernel, alpha=alpha), a, w, [x, gate],
        [pl.BlockSpec((tm, tn), lambda i, j, kk: (i, j)),
         pl.BlockSpec((None, 1, tn), lambda i, j, kk: ((i * tm) // seq, 0, j))],
        jax.ShapeDtypeStruct((m, n), F32),
        pl.BlockSpec((tm, tn), lambda i, j, kk: (i, j)),
        tiles, n, 0, "gemm_resid")


def _rope_specs(tm, seq):
    nrow = seq // tm
    return [pl.BlockSpec((tm, LANES), lambda i, j, kk: (i % nrow, 0))] * 2


def _gemm_proj_a(h, w_in, tables, seq):
    m, k = h.shape
    tiles = _gemm_tiles(m, PROJ_A, k, seq)
    tm, tn, _ = tiles
    heads = tn // HEAD_DIM
    return _gemm_call(
        functools.partial(_gemm_proj_a_kernel, rope_tiles=2 * WIDTH_A // tn), h, w_in, list(tables),
        _rope_specs(tm, seq),
        jax.ShapeDtypeStruct((PROJ_A // HEAD_DIM, m, HEAD_DIM), F32),
        pl.BlockSpec((heads, tm, HEAD_DIM), lambda i, j, kk: (j, i, 0)),
        tiles, PROJ_A, 0, "gemm_proj_a")


def _gemm_proj_bq(h, w_in, tables, g_q, seq):
    m, k = h.shape
    tiles = _gemm_tiles(m, WIDTH_B, k, seq)
    tm, tn, _ = tiles
    heads = tn // HEAD_DIM
    return _gemm_call(
        _gemm_proj_bq_kernel, h, w_in, list(tables) + [g_q.reshape(1, HEAD_DIM)],
        _rope_specs(tm, seq) + [pl.BlockSpec((1, HEAD_DIM), lambda i, j, kk: (0, 0))],
        jax.ShapeDtypeStruct((N_HEADS_B, m, HEAD_DIM), BF16),
        pl.BlockSpec((heads, tm, HEAD_DIM), lambda i, j, kk: (j, i, 0)),
        tiles, WIDTH_B, PROJ_A, "gemm_proj_bq")


def _gemm_proj_bkv(h, w_in, tables, g_k, seq):
    m, k = h.shape
    tm, _, tk = _gemm_tiles(m, 2 * KV_WIDTH_B, k, seq)
    tiles = (tm, 2 * KV_WIDTH_B, tk)
    return _gemm_call(
        _gemm_proj_bkv_kernel, h, w_in, list(tables) + [g_k.reshape(1, HEAD_DIM)],
        _rope_specs(tm, seq) + [pl.BlockSpec((1, HEAD_DIM), lambda i, j, kk: (0, 0))],
        (jax.ShapeDtypeStruct((N_KV_B, m, HEAD_DIM), BF16), jax.ShapeDtypeStruct((N_KV_B, HEAD_DIM, m), BF16)),
        (pl.BlockSpec((N_KV_B, tm, HEAD_DIM), lambda i, j, kk: (0, i, 0)),
         pl.BlockSpec((N_KV_B, HEAD_DIM, tm), lambda i, j, kk: (0, 0, i))),
        tiles, 2 * KV_WIDTH_B, PROJ_A + WIDTH_B, "gemm_proj_bkv")


def _mixer_a_kernel(q_ref, k_ref, v_ref, o_ref, acc_ref, m_ref, d_ref, *, seq, window, plan):
    win = pl.program_id(2)

    for branch, (dil, tq) in enumerate(plan):
        length = seq // dil
        tkv = tq + 2 * HALF_WINDOW
        blocks = window // (dil * tq)
        offset = (lax.broadcasted_iota(jnp.int32, (tq, tkv), 1)
                  - lax.broadcasted_iota(jnp.int32, (tq, tkv), 0))

        def rows(start, size, dil=dil):
            return pl.ds(start, size) if dil == 1 else pl.ds(start, size, stride=dil)

        for res in range(dil):
            for blk in range(blocks):
                loc = res + dil * tq * blk
                q0 = win * (window // dil) + tq * blk
                k0 = jnp.clip(q0 - HALF_WINDOW, 0, length - tkv)
                q = (q_ref[rows(loc, tq), :] * (ATTN_SCALE * LOG2_E)).astype(BF16)
                k = k_ref[rows(res + dil * k0, tkv), :].astype(BF16)
                v = v_ref[rows(res + dil * k0, tkv), :].astype(BF16)
                s = lax.dot_general(q, k, (((1,), (1,)), ((), ())), preferred_element_type=F32)
                shifted = offset + (k0 - q0 + HALF_WINDOW)
                valid = lax.bitcast_convert_type(shifted, jnp.uint32) <= 2 * HALF_WINDOW
                s = jnp.where(valid, s, NEG_BIG)
                m = jnp.max(s, axis=-1, keepdims=True)
                p = jnp.exp2(s - m).astype(BF16)
                v_ones = jnp.concatenate([v, jnp.ones((tkv, LANES), BF16)], axis=1)
                pv = jnp.dot(p, v_ones, preferred_element_type=F32)
                acc_ref[branch, rows(loc, tq), :] = pv[:, :HEAD_DIM]
                m_ref[branch, rows(loc, tq), :] = jnp.broadcast_to(m, (tq, LANES))
                d_ref[branch, rows(loc, tq), :] = pv[:, HEAD_DIM:]

    m_all = m_ref[0]
    for branch in range(1, len(plan)):
        m_all = jnp.maximum(m_all, m_ref[branch])
    num = jnp.zeros(o_ref.shape, F32)
    tot = jnp.zeros(o_ref.shape, F32)
    for branch in range(len(plan)):
        w = jnp.exp2(m_ref[branch] - m_all)
        num = num + w * acc_ref[branch]
        tot = tot + w * d_ref[branch]
    o_ref[...] = num / tot


def _mixer_a_plan(seq, window):
    plan = []
    for dil in DILATIONS:
        length = seq // dil
        tq = min(128, length // 2, window // dil)
        assert length >= tq + 2 * HALF_WINDOW and window % (dil * tq) == 0, (seq, window, dil)
        plan.append((dil, tq))
    return tuple(plan)


def _mixer_a(qkv, batch, seq):
    rows = batch * seq
    window = min(2048, seq)
    nwin = seq // window
    plan = _mixer_a_plan(seq, window)
    q_spec = pl.BlockSpec((None, window, HEAD_DIM), lambda b, h, w: (h, b * nwin + w, 0))
    k_spec = pl.BlockSpec((None, seq, HEAD_DIM), lambda b, h, w: (N_HEADS_A + h, b, 0))
    v_spec = pl.BlockSpec((None, seq, HEAD_DIM), lambda b, h, w: (2 * N_HEADS_A + h, b, 0))
    stat = pltpu.VMEM((len(plan), window, LANES), F32)
    return pl.pallas_call(
        functools.partial(_mixer_a_kernel, seq=seq, window=window, plan=plan),
        out_shape=jax.ShapeDtypeStruct((rows, WIDTH_A), F32),
        grid=(batch, N_HEADS_A, nwin),
        in_specs=[q_spec, k_spec, v_spec],
        out_specs=pl.BlockSpec((window, HEAD_DIM), lambda b, h, w: (b * nwin + w, h)),
        scratch_shapes=[stat, stat, stat],
        compiler_params=_params(("parallel", "parallel", "arbitrary")),
        name="mixer_a_dilated",
    )(qkv, qkv, qkv)


SUBLANES = 8
BF16_SUBLANES = 16
MIXER_B_CHAIN_COLS = 256


def _reduce_rows(x, op):
    parts = [x[r:r + SUBLANES] for r in range(0, x.shape[0], SUBLANES)]
    while len(parts) > 1:
        parts = [op(parts[i], parts[i + 1]) for i in range(0, len(parts) - 1, 2)] + (
            [parts[-1]] if len(parts) % 2 else [])
    top = parts[0]
    rows = [top[r:r + 1] for r in range(top.shape[0])]
    while len(rows) > 1:
        rows = [op(rows[i], rows[i + 1]) for i in range(0, len(rows), 2)]
    return rows[0]


def _mixer_b_kernel(q_ref, k_ref, vt_ref, o_ref, m_ref, acc_ref, *, nkv, chains):
    ki = pl.program_id(3)

    @pl.when(ki == 0)
    def _():
        m_ref[...] = jnp.full_like(m_ref, NEG_BIG)
        acc_ref[...] = jnp.zeros_like(acc_ref)

    k = k_ref[...]
    ones = jnp.ones((acc_ref.shape[1] - HEAD_DIM, k.shape[0]), BF16)
    vt = jnp.concatenate([vt_ref[...], ones], axis=0)
    tq = q_ref.shape[1]
    cols = m_ref.shape[-1]
    chains_per_head = tq // cols

    def scores_and_max(c):
        q = q_ref[c // chains_per_head, pl.ds((c % chains_per_head) * cols, cols), :]
        st = lax.dot_general(k, q, (((1,), (1,)), ((), ())), preferred_element_type=F32)
        return st, jnp.maximum(m_ref[c], _reduce_rows(st, jnp.maximum))

    def accumulate(c, st, m_new):
        alpha = jnp.exp2(m_ref[c] - m_new)
        p = jnp.exp2(st - m_new).astype(BF16)
        acc_ref[c] = alpha * acc_ref[c] + jnp.dot(vt, p, preferred_element_type=F32)
        m_ref[c] = m_new

    pending = [scores_and_max(c) for c in range(chains)]
    for c in range(chains):
        accumulate(c, *pending[c])

    @pl.when(ki == nkv - 1)
    def _():
        for c in range(chains):
            head, part = divmod(c, chains_per_head)
            acc = acc_ref[c]
            o_ref[part * cols:(part + 1) * cols, head * HEAD_DIM:(head + 1) * HEAD_DIM] = (
                acc[:HEAD_DIM] / acc[HEAD_DIM:HEAD_DIM + 1]).T


def _mixer_b(q, k, vt, batch, seq):
    rows = batch * seq
    tq = min(1024, seq)
    tkv = min(2048, seq)
    nq, nkv = seq // tq, seq // tkv
    cols = min(MIXER_B_CHAIN_COLS, tq)
    chains = GQA_GROUP * tq // cols
    q_spec = pl.BlockSpec((GQA_GROUP, tq, HEAD_DIM), lambda b, g, qi, ki: (g, b * nq + qi, 0))
    k_spec = pl.BlockSpec((None, tkv, HEAD_DIM), lambda b, g, qi, ki: (g, b * nkv + ki, 0))
    vt_spec = pl.BlockSpec((None, HEAD_DIM, tkv), lambda b, g, qi, ki: (g, 0, b * nkv + ki))
    return pl.pallas_call(
        functools.partial(_mixer_b_kernel, nkv=nkv, chains=chains),
        out_shape=jax.ShapeDtypeStruct((rows, WIDTH_B), F32),
        grid=(batch, N_KV_B, nq, nkv),
        in_specs=[q_spec, k_spec, vt_spec],
        out_specs=pl.BlockSpec((tq, GQA_GROUP * HEAD_DIM), lambda b, g, qi, ki: (b * nq + qi, g)),
        scratch_shapes=[
            pltpu.VMEM((chains, 1, cols), F32),
            pltpu.VMEM((chains, HEAD_DIM + BF16_SUBLANES, cols), F32),
        ],
        compiler_params=_params(("parallel", "parallel", "parallel", "arbitrary")),
        name="mixer_b_gqa",
    )(q, k, vt)


def _paired_layout(rotary_halves):
    lower = [d for a, _, w in rotary_halves for d in range(a, a + w)]
    upper = [d for _, b, w in rotary_halves for d in range(b, b + w)]
    rest = [d for d in range(HEAD_DIM) if d not in lower and d not in upper]
    lower += rest[:len(rest) // 2]
    upper += rest[len(rest) // 2:]
    return np.array(lower + upper)


PERM_A = _paired_layout([(0, PARTIAL_ROT_DIM // 2, PARTIAL_ROT_DIM // 2)])
PERM_B = _paired_layout([(0, HEAD_DIM // 4, HEAD_DIM // 4), (HEAD_DIM // 2, 3 * HEAD_DIM // 4, HEAD_DIM // 4)])


def _rotary_tables(angle_groups):
    seq = angle_groups[0].shape[0]
    cos = np.ones((seq, HEAD_DIM), np.float64)
    sin = np.zeros((seq, HEAD_DIM), np.float64)
    lane = 0
    for ang in angle_groups:
        w = ang.shape[1]
        for base, sign in ((lane, -1.0), (lane + HEAD_DIM // 2, 1.0)):
            cos[:, base:base + w] = np.cos(ang)
            sin[:, base:base + w] = sign * np.sin(ang)
        lane += w
    return tuple(jnp.asarray(t, F32) for t in (cos, sin))


def _inv_freq(rot_dim, theta):
    return theta ** (-np.arange(0, rot_dim, 2, dtype=np.float64) / rot_dim)


def _partial_rope_tables(seq):
    pos = np.arange(seq, dtype=np.float64)
    return _rotary_tables([pos[:, None] * _inv_freq(PARTIAL_ROT_DIM, ROPE_THETA)[None, :]])


def _axial_rope_tables(seq):
    pos = np.arange(seq)
    inv = _inv_freq(HEAD_DIM // 2, AXIAL_THETA)[None, :]
    row = (pos // GRID_W).astype(np.float64)[:, None]
    col = (pos % GRID_W).astype(np.float64)[:, None]
    return _rotary_tables([row * inv, col * inv])


def _run_group(x, mod, weights, depth):
    batch, seq, d = x.shape
    rows = batch * seq
    alpha = (2 * depth) ** 0.25
    tables_a = _partial_rope_tables(seq)
    tables_b = _axial_rope_tables(seq)
    x = x.reshape(rows, d)

    def chunks(l):
        m = mod[l].reshape(batch, 6, 1, d)
        return [m[:, i] for i in range(6)]

    sh1, sc1, g1, sh2, sc2, g2 = chunks(0)
    h = _modulate(x, sc1, sh1, seq)
    for l in range(depth):
        w = weights[l]
        qkv_a = _gemm_proj_a(h, w["w_in"], tables_a, seq)
        q_b = _gemm_proj_bq(h, w["w_in"], tables_b, w["g_q"], seq)
        k_b, vt_b = _gemm_proj_bkv(h, w["w_in"], tables_b, w["g_k"], seq)
        o_a = _mixer_a(qkv_a, batch, seq)
        o_b = _mixer_b(q_b, k_b, vt_b, batch, seq)
        mixed = _norm_mix(o_a, o_b, w["g_out_a"], w["g_out_b"], seq)
        y = _gemm_resid(mixed, w["w_out"], x, g1, seq, alpha)
        x, h = _layernorm_modulate(y, w["ln1_g"], w["ln1_b"], sc2, sh2, seq)
        f = _gemm_relu2(h, w["w_up"])
        y = _gemm_resid(f, w["w_down"], x, g2, seq, alpha)
        if l + 1 < depth:
            sh1, sc1, g1, sh2, sc2, g2 = chunks(l + 1)
            x, h = _layernorm_modulate(y, w["ln2_g"], w["ln2_b"], sc1, sh1, seq)
        else:
            x = _layernorm(y, w["ln2_g"], w["ln2_b"], seq)
    return x.reshape(batch, seq, d)


def _permute_lanes(y, perm):
    lane = lax.broadcasted_iota(jnp.int32, y.shape, 1)
    shifts = (np.arange(HEAD_DIM) - perm) % HEAD_DIM
    out = y
    for shift in sorted(set(shifts.tolist()) - {0}):
        lanes = np.nonzero(shifts == shift)[0]
        lo, hi = int(lanes[0]), int(lanes[-1])
        assert hi - lo + 1 == lanes.size
        out = jnp.where(jnp.logical_and(lane >= lo, lane <= hi), pltpu.roll(y, shift, 1), out)
    return out


def _prep_w_in_kernel(w_ref, o_ref):
    qk_a_heads = 2 * N_HEADS_A
    b_first = PROJ_A // HEAD_DIM
    qk_b_heads = N_HEADS_B + N_KV_B
    for head in range(o_ref.shape[-1] // HEAD_DIM):
        cols = slice(head * HEAD_DIM, (head + 1) * HEAD_DIM)
        y = w_ref[:, cols]
        if head < qk_a_heads:
            y = _permute_lanes(y, PERM_A)
        elif b_first <= head < b_first + qk_b_heads:
            y = _permute_lanes(y, PERM_B)
        o_ref[:, cols] = y.astype(o_ref.dtype)


def _prep_w_in(w_in):
    depth, d, n = w_in.shape
    tk = min(256, d)
    return pl.pallas_call(
        _prep_w_in_kernel,
        out_shape=jax.ShapeDtypeStruct((depth, d, n), BF16),
        grid=(depth, d // tk),
        in_specs=[pl.BlockSpec((None, tk, n), lambda l, i: (l, i, 0))],
        out_specs=pl.BlockSpec((None, tk, n), lambda l, i: (l, i, 0)),
        compiler_params=_params(("parallel", "parallel")),
        name="prep_w_in",
    )(w_in)


def kernel(x_prompt, x_sample, c_prompt, c_sample, w_ada, b_ada, w_in, g_q, g_k, g_out_a, g_out_b, w_out,
           ln1_g, ln1_b, w_up, w_down, ln2_g, ln2_b):
    depth = w_ada.shape[0]
    n_prompt, n_sample = c_prompt.shape[0], c_sample.shape[0]
    pad = -(n_prompt + n_sample) % 8
    c_rows = jnp.concatenate([c_prompt, c_sample, jnp.zeros((pad, c_prompt.shape[1]), F32)], axis=0)
    mod = _ada(c_rows, w_ada, b_ada)

    w_in = _prep_w_in(w_in)
    g_q, g_k = g_q[:, PERM_B], g_k[:, PERM_B]
    w_out, w_up, w_down = (w.astype(BF16) for w in (w_out, w_up, w_down))
    weights = [
        dict(w_in=(w_in, l), w_out=(w_out, l), w_up=(w_up, l), w_down=(w_down, l),
             g_q=g_q[l], g_k=g_k[l], g_out_a=g_out_a[l], g_out_b=g_out_b[l],
             ln1_g=ln1_g[l], ln1_b=ln1_b[l], ln2_g=ln2_g[l], ln2_b=ln2_b[l])
        for l in range(depth)
    ]
    y_prompt = _run_group(x_prompt, mod[:, :n_prompt], weights, depth)
    y_sample = _run_group(x_sample, mod[:, n_prompt:n_prompt + n_sample], weights, depth)
    return (y_prompt, y_sample)
```

```python
import functools

import numpy as np
import jax
import jax.numpy as jnp
from jax import lax
from jax.experimental import pallas as pl
from jax.experimental.pallas import tpu as pltpu

HEAD_DIM = 128
N_HEADS_A = 16
N_HEADS_B = 16
N_KV_B = 4
GQA_GROUP = N_HEADS_B // N_KV_B
WIDTH_A = N_HEADS_A * HEAD_DIM
WIDTH_B = N_HEADS_B * HEAD_DIM
KV_WIDTH_B = N_KV_B * HEAD_DIM
PROJ_A = 3 * WIDTH_A
PROJ_B = WIDTH_B + 2 * KV_WIDTH_B
DILATIONS = (1, 4, 16)
HALF_WINDOW = 64
PARTIAL_ROT_DIM = HEAD_DIM // 4
ROPE_THETA = 500000.0
AXIAL_THETA = 10000.0
GRID_W = 64
LN_EPS = 1e-5
RMS_EPS = 1e-6
NEG_BIG = -1e30
ATTN_SCALE = HEAD_DIM ** -0.5
LOG2_E = 1.4426950408889634

V7X_VMEM_BYTES = 64 * 1024 * 1024
VMEM_LIMIT_BYTES = V7X_VMEM_BYTES - 6 * 1024 * 1024
LANES = 128

F32 = jnp.float32
BF16 = jnp.bfloat16


def _params(semantics):
    return pltpu.CompilerParams(dimension_semantics=semantics, vmem_limit_bytes=VMEM_LIMIT_BYTES)


def _ada_kernel(c_ref, w_ref, b_ref, o_ref):
    c = c_ref[...]
    s = (c * (1.0 / (1.0 + jnp.exp(-c)))).astype(BF16)
    w = w_ref[...].astype(BF16)
    o_ref[...] = jnp.dot(s, w, preferred_element_type=F32) + b_ref[...]


def _ada(c_rows, w_ada, b_ada):
    depth, d, n = w_ada.shape
    rows = c_rows.shape[0]
    tn = min(512, n)
    return pl.pallas_call(
        _ada_kernel,
        out_shape=jax.ShapeDtypeStruct((depth, rows, n), F32),
        grid=(depth, n // tn),
        in_specs=[
            pl.BlockSpec((rows, d), lambda l, j: (0, 0)),
            pl.BlockSpec((None, d, tn), lambda l, j: (l, 0, j)),
            pl.BlockSpec((None, 1, tn), lambda l, j: (l, 0, j)),
        ],
        out_specs=pl.BlockSpec((None, rows, tn), lambda l, j: (l, 0, j)),
        compiler_params=_params(("parallel", "parallel")),
        name="ada_modulation",
    )(c_rows, w_ada, b_ada.reshape(depth, 1, n))


def _modulate_kernel(x_ref, sc_ref, sh_ref, h_ref):
    h_ref[...] = (x_ref[...] * (1.0 + sc_ref[...]) + sh_ref[...]).astype(BF16)


def _layernorm_rows(y, g, b):
    mu = jnp.mean(y, axis=-1, keepdims=True)
    yc = y - mu
    var = jnp.mean(yc * yc, axis=-1, keepdims=True)
    return yc * lax.rsqrt(var + LN_EPS) * g + b


def _ln_kernel(y_ref, g_ref, b_ref, x_ref):
    x_ref[...] = _layernorm_rows(y_ref[...], g_ref[...], b_ref[...])


def _ln_modulate_kernel(y_ref, g_ref, b_ref, sc_ref, sh_ref, x_ref, h_ref):
    x = _layernorm_rows(y_ref[...], g_ref[...], b_ref[...])
    x_ref[...] = x
    h_ref[...] = (x * (1.0 + sc_ref[...]) + sh_ref[...]).astype(BF16)


def _rms_rows(o, g):
    return o * lax.rsqrt(jnp.mean(o * o, axis=-1, keepdims=True) + RMS_EPS) * g


def _norm_mix_kernel(oa_ref, ob_ref, ga_ref, gb_ref, m_ref):
    wa = oa_ref.shape[-1]
    m_ref[:, :wa] = _rms_rows(oa_ref[...], ga_ref[...]).astype(BF16)
    m_ref[:, wa:] = _rms_rows(ob_ref[...], gb_ref[...]).astype(BF16)


def _row_tile(rows, seq):
    return min(256, seq, rows)


def _modulate(x, sc, sh, seq):
    rows, d = x.shape
    tm = _row_tile(rows, seq)
    per_batch = lambda i: ((i * tm) // seq, 0, 0)
    return pl.pallas_call(
        _modulate_kernel,
        out_shape=jax.ShapeDtypeStruct((rows, d), BF16),
        grid=(rows // tm,),
        in_specs=[
            pl.BlockSpec((tm, d), lambda i: (i, 0)),
            pl.BlockSpec((None, 1, d), per_batch),
            pl.BlockSpec((None, 1, d), per_batch),
        ],
        out_specs=pl.BlockSpec((tm, d), lambda i: (i, 0)),
        compiler_params=_params(("parallel",)),
        name="modulate",
    )(x, sc, sh)


def _layernorm(y, g, b, seq):
    rows, d = y.shape
    tm = _row_tile(rows, seq)
    return pl.pallas_call(
        _ln_kernel,
        out_shape=jax.ShapeDtypeStruct((rows, d), F32),
        grid=(rows // tm,),
        in_specs=[
            pl.BlockSpec((tm, d), lambda i: (i, 0)),
            pl.BlockSpec((1, d), lambda i: (0, 0)),
            pl.BlockSpec((1, d), lambda i: (0, 0)),
        ],
        out_specs=pl.BlockSpec((tm, d), lambda i: (i, 0)),
        compiler_params=_params(("parallel",)),
        name="layernorm",
    )(y, g.reshape(1, d), b.reshape(1, d))


def _layernorm_modulate(y, g, b, sc, sh, seq):
    rows, d = y.shape
    tm = _row_tile(rows, seq)
    per_batch = lambda i: ((i * tm) // seq, 0, 0)
    return pl.pallas_call(
        _ln_modulate_kernel,
        out_shape=(jax.ShapeDtypeStruct((rows, d), F32), jax.ShapeDtypeStruct((rows, d), BF16)),
        grid=(rows // tm,),
        in_specs=[
            pl.BlockSpec((tm, d), lambda i: (i, 0)),
            pl.BlockSpec((1, d), lambda i: (0, 0)),
            pl.BlockSpec((1, d), lambda i: (0, 0)),
            pl.BlockSpec((None, 1, d), per_batch),
            pl.BlockSpec((None, 1, d), per_batch),
        ],
        out_specs=(pl.BlockSpec((tm, d), lambda i: (i, 0)), pl.BlockSpec((tm, d), lambda i: (i, 0))),
        compiler_params=_params(("parallel",)),
        name="layernorm_modulate",
    )(y, g.reshape(1, d), b.reshape(1, d), sc, sh)


def _norm_mix(o_a, o_b, g_a, g_b, seq):
    rows, wa = o_a.shape
    wb = o_b.shape[1]
    tm = _row_tile(rows, seq)
    return pl.pallas_call(
        _norm_mix_kernel,
        out_shape=jax.ShapeDtypeStruct((rows, wa + wb), BF16),
        grid=(rows // tm,),
        in_specs=[
            pl.BlockSpec((tm, wa), lambda i: (i, 0)),
            pl.BlockSpec((tm, wb), lambda i: (i, 0)),
            pl.BlockSpec((1, wa), lambda i: (0, 0)),
            pl.BlockSpec((1, wb), lambda i: (0, 0)),
        ],
        out_specs=pl.BlockSpec((tm, wa + wb), lambda i: (i, 0)),
        compiler_params=_params(("parallel",)),
        name="norm_mix",
    )(o_a, o_b, g_a.reshape(1, wa), g_b.reshape(1, wb))


def _accumulate(a_ref, w_ref, acc_ref, nk, epilogue):
    def part():
        return jnp.dot(a_ref[...], w_ref[...], preferred_element_type=F32)

    if nk == 1:
        epilogue(part())
        return
    k = pl.program_id(2)

    @pl.when(k == 0)
    def _():
        acc_ref[...] = part()

    @pl.when(jnp.logical_and(k > 0, k < nk - 1))
    def _():
        acc_ref[...] += part()

    @pl.when(k == nk - 1)
    def _():
        epilogue(acc_ref[...] + part())


def _rotate_pairs(y, cos, sin):
    return y * cos + pltpu.roll(y, HEAD_DIM // 2, 1) * sin


def _gemm_relu2_kernel(a_ref, w_ref, o_ref, *scratch, nk):
    def epilogue(acc):
        o_ref[...] = jnp.square(jnp.maximum(acc, 0.0)).astype(o_ref.dtype)
    _accumulate(a_ref, w_ref, scratch[0] if scratch else None, nk, epilogue)


def _gemm_resid_kernel(a_ref, w_ref, x_ref, gate_ref, o_ref, *scratch, nk, alpha):
    def epilogue(acc):
        o_ref[...] = alpha * x_ref[...] + gate_ref[...] * acc
    _accumulate(a_ref, w_ref, scratch[0] if scratch else None, nk, epilogue)


def _gemm_proj_a_kernel(a_ref, w_ref, cos_ref, sin_ref, o_ref, *scratch, nk, rope_tiles):
    heads = o_ref.shape[0]

    def epilogue(acc):
        j = pl.program_id(1)

        @pl.when(j < rope_tiles)
        def _():
            for hh in range(heads):
                y = acc[:, hh * HEAD_DIM:(hh + 1) * HEAD_DIM]
                o_ref[hh] = _rotate_pairs(y, cos_ref[...], sin_ref[...])

        @pl.when(j >= rope_tiles)
        def _():
            for hh in range(heads):
                o_ref[hh] = acc[:, hh * HEAD_DIM:(hh + 1) * HEAD_DIM]

    _accumulate(a_ref, w_ref, scratch[0] if scratch else None, nk, epilogue)


def _axial_normed_rotated(y, g_ref, cos_ref, sin_ref):
    return _rotate_pairs(_rms_rows(y, g_ref[...]), cos_ref[...], sin_ref[...])


def _gemm_proj_bq_kernel(a_ref, w_ref, cos_ref, sin_ref, gq_ref, o_ref, *scratch, nk):
    def epilogue(acc):
        for hh in range(o_ref.shape[0]):
            y = acc[:, hh * HEAD_DIM:(hh + 1) * HEAD_DIM]
            q = _axial_normed_rotated(y, gq_ref, cos_ref, sin_ref)
            o_ref[hh] = (q * (ATTN_SCALE * LOG2_E)).astype(o_ref.dtype)

    _accumulate(a_ref, w_ref, scratch[0] if scratch else None, nk, epilogue)


def _gemm_proj_bkv_kernel(a_ref, w_ref, cos_ref, sin_ref, gk_ref, k_ref, vt_ref, *scratch, nk):
    def epilogue(acc):
        for hh in range(N_KV_B):
            y = acc[:, hh * HEAD_DIM:(hh + 1) * HEAD_DIM]
            k_ref[hh] = _axial_normed_rotated(y, gk_ref, cos_ref, sin_ref).astype(k_ref.dtype)
        for hh in range(N_KV_B):
            y = acc[:, (N_KV_B + hh) * HEAD_DIM:(N_KV_B + hh + 1) * HEAD_DIM]
            vt_ref[hh] = y.T.astype(vt_ref.dtype)

    _accumulate(a_ref, w_ref, scratch[0] if scratch else None, nk, epilogue)


def _gemm_tiles(m, n, k, seq):
    tm = min(1024, seq, m)
    tn = min(1024, n)
    tk = k if k <= 4096 else 2048
    return tm, tn, tk


def _gemm_call(kernel_fn, a, w, extra_inputs, extra_specs, out_shape, out_spec, tiles, n_cols, col_offset, name):
    w_all, layer = w
    m, k = a.shape
    tm, tn, tk = tiles
    nk = k // tk
    joff = col_offset // tn
    scratch = [pltpu.VMEM((tm, tn), F32)] if nk > 1 else []
    return pl.pallas_call(
        functools.partial(kernel_fn, nk=nk),
        out_shape=out_shape,
        grid=(m // tm, n_cols // tn, nk),
        in_specs=[
            pl.BlockSpec((tm, tk), lambda i, j, kk: (i, kk)),
            pl.BlockSpec((None, tk, tn), lambda i, j, kk: (layer, kk, j + joff)),
        ] + extra_specs,
        out_specs=out_spec,
        scratch_shapes=scratch,
        compiler_params=_params(("parallel", "parallel", "arbitrary")),
        name=name,
    )(a, w_all, *extra_inputs)


def _gemm_relu2(a, w):
    m, k = a.shape
    n = w[0].shape[2]
    tiles = _gemm_tiles(m, n, k, m)
    tm, tn, _ = tiles
    return _gemm_call(
        _gemm_relu2_kernel, a, w, [], [],
        jax.ShapeDtypeStruct((m, n), BF16),
        pl.BlockSpec((tm, tn), lambda i, j, kk: (i, j)),
        tiles, n, 0, "gemm_relu2")


def _gemm_resid(a, w, x, gate, seq, alpha):
    m, k = a.shape
    n = w[0].shape[2]
    tiles = _gemm_tiles(m, n, k, seq)
    tm, tn, _ = tiles
    return _gemm_call(
        functools.partial(_gemm_resid_kernel, alpha=alpha), a, w, [x, gate],
        [pl.BlockSpec((tm, tn), lambda i, j, kk: (i, j)),
         pl.BlockSpec((None, 1, tn), lambda i, j, kk: ((i * tm) // seq, 0, j))],
        jax.ShapeDtypeStruct((m, n), F32),
        pl.BlockSpec((tm, tn), lambda i, j, kk: (i, j)),
        tiles, n, 0, "gemm_resid")


def _rope_specs(tm, seq):
    nrow = seq // tm
    return [pl.BlockSpec((tm, LANES), lambda i, j, kk: (i % nrow, 0))] * 2


def _gemm_proj_a(h, w_in, tables, seq):
    m, k = h.shape
    tiles = _gemm_tiles(m, PROJ_A, k, seq)
    tm, tn, _ = tiles
    heads = tn // HEAD_DIM
    return _gemm_call(
        functools.partial(_gemm_proj_a_kernel, rope_tiles=2 * WIDTH_A // tn), h, w_in, list(tables),
        _rope_specs(tm, seq),
        jax.ShapeDtypeStruct((PROJ_A // HEAD_DIM, m, HEAD_DIM), F32),
        pl.BlockSpec((heads, tm, HEAD_DIM), lambda i, j, kk: (j, i, 0)),
        tiles, PROJ_A, 0, "gemm_proj_a")


def _gemm_proj_bq(h, w_in, tables, g_q, seq):
    m, k = h.shape
    tiles = _gemm_tiles(m, WIDTH_B, k, seq)
    tm, tn, _ = tiles
    heads = tn // HEAD_DIM
    return _gemm_call(
        _gemm_proj_bq_kernel, h, w_in, list(tables) + [g_q.reshape(1, HEAD_DIM)],
        _rope_specs(tm, seq) + [pl.BlockSpec((1, HEAD_DIM), lambda i, j, kk: (0, 0))],
        jax.ShapeDtypeStruct((N_HEADS_B, m, HEAD_DIM), BF16),
        pl.BlockSpec((heads, tm, HEAD_DIM), lambda i, j, kk: (j, i, 0)),
        tiles, WIDTH_B, PROJ_A, "gemm_proj_bq")


def _gemm_proj_bkv(h, w_in, tables, g_k, seq):
    m, k = h.shape
    tm, _, tk = _gemm_tiles(m, 2 * KV_WIDTH_B, k, seq)
    tiles = (tm, 2 * KV_WIDTH_B, tk)
    return _gemm_call(
        _gemm_proj_bkv_kernel, h, w_in, list(tables) + [g_k.reshape(1, HEAD_DIM)],
        _rope_specs(tm, seq) + [pl.BlockSpec((1, HEAD_DIM), lambda i, j, kk: (0, 0))],
        (jax.ShapeDtypeStruct((N_KV_B, m, HEAD_DIM), BF16), jax.ShapeDtypeStruct((N_KV_B, HEAD_DIM, m), BF16)),
        (pl.BlockSpec((N_KV_B, tm, HEAD_DIM), lambda i, j, kk: (0, i, 0)),
         pl.BlockSpec((N_KV_B, HEAD_DIM, tm), lambda i, j, kk: (0, 0, i))),
        tiles, 2 * KV_WIDTH_B, PROJ_A + WIDTH_B, "gemm_proj_bkv")


def _mixer_a_kernel(q_ref, k_ref, v_ref, o_ref, acc_ref, m_ref, d_ref, *, seq, window, plan):
    win = pl.program_id(2)

    for branch, (dil, tq) in enumerate(plan):
        length = seq // dil
        tkv = tq + 2 * HALF_WINDOW
        blocks = window // (dil * tq)
        offset = (lax.broadcasted_iota(jnp.int32, (tq, tkv), 1)
                  - lax.broadcasted_iota(jnp.int32, (tq, tkv), 0))

        def rows(start, size, dil=dil):
            return pl.ds(start, size) if dil == 1 else pl.ds(start, size, stride=dil)

        for res in range(dil):
            for blk in range(blocks):
                loc = res + dil * tq * blk
                q0 = win * (window // dil) + tq * blk
                k0 = jnp.clip(q0 - HALF_WINDOW, 0, length - tkv)
                q = (q_ref[rows(loc, tq), :] * (ATTN_SCALE * LOG2_E)).astype(BF16)
                k = k_ref[rows(res + dil * k0, tkv), :].astype(BF16)
                v = v_ref[rows(res + dil * k0, tkv), :].astype(BF16)
                s = lax.dot_general(q, k, (((1,), (1,)), ((), ())), preferred_element_type=F32)
                shifted = offset + (k0 - q0 + HALF_WINDOW)
                valid = lax.bitcast_convert_type(shifted, jnp.uint32) <= 2 * HALF_WINDOW
                s = jnp.where(valid, s, NEG_BIG)
                m = jnp.max(s, axis=-1, keepdims=True)
                p = jnp.exp2(s - m).astype(BF16)
                v_ones = jnp.concatenate([v, jnp.ones((tkv, LANES), BF16)], axis=1)
                pv = jnp.dot(p, v_ones, preferred_element_type=F32)
                acc_ref[branch, rows(loc, tq), :] = pv[:, :HEAD_DIM]
                m_ref[branch, rows(loc, tq), :] = jnp.broadcast_to(m, (tq, LANES))
                d_ref[branch, rows(loc, tq), :] = pv[:, HEAD_DIM:]

    m_all = m_ref[0]
    for branch in range(1, len(plan)):
        m_all = jnp.maximum(m_all, m_ref[branch])
    num = jnp.zeros(o_ref.shape, F32)
    tot = jnp.zeros(o_ref.shape, F32)
    for branch in range(len(plan)):
        w = jnp.exp2(m_ref[branch] - m_all)
        num = num + w * acc_ref[branch]
        tot = tot + w * d_ref[branch]
    o_ref[...] = num / tot


def _mixer_a_plan(seq, window):
    plan = []
    for dil in DILATIONS:
        length = seq // dil
        tq = min(128, length // 2, window // dil)
        assert length >= tq + 2 * HALF_WINDOW and window % (dil * tq) == 0, (seq, window, dil)
        plan.append((dil, tq))
    return tuple(plan)


def _mixer_a(qkv, batch, seq):
    rows = batch * seq
    window = min(2048, seq)
    nwin = seq // window
    plan = _mixer_a_plan(seq, window)
    q_spec = pl.BlockSpec((None, window, HEAD_DIM), lambda b, h, w: (h, b * nwin + w, 0))
    k_spec = pl.BlockSpec((None, seq, HEAD_DIM), lambda b, h, w: (N_HEADS_A + h, b, 0))
    v_spec = pl.BlockSpec((None, seq, HEAD_DIM), lambda b, h, w: (2 * N_HEADS_A + h, b, 0))
    stat = pltpu.VMEM((len(plan), window, LANES), F32)
    return pl.pallas_call(
        functools.partial(_mixer_a_kernel, seq=seq, window=window, plan=plan),
        out_shape=jax.ShapeDtypeStruct((rows, WIDTH_A), F32),
        grid=(batch, N_HEADS_A, nwin),
        in_specs=[q_spec, k_spec, v_spec],
        out_specs=pl.BlockSpec((window, HEAD_DIM), lambda b, h, w: (b * nwin + w, h)),
        scratch_shapes=[stat, stat, stat],
        compiler_params=_params(("parallel", "parallel", "arbitrary")),
        name="mixer_a_dilated",
    )(qkv, qkv, qkv)


SUBLANES = 8
BF16_SUBLANES = 16
MIXER_B_CHAIN_COLS = 256


def _reduce_rows(x, op):
    parts = [x[r:r + SUBLANES] for r in range(0, x.shape[0], SUBLANES)]
    while len(parts) > 1:
        parts = [op(parts[i], parts[i + 1]) for i in range(0, len(parts) - 1, 2)] + (
            [parts[-1]] if len(parts) % 2 else [])
    top = parts[0]
    rows = [top[r:r + 1] for r in range(top.shape[0])]
    while len(rows) > 1:
        rows = [op(rows[i], rows[i + 1]) for i in range(0, len(rows), 2)]
    return rows[0]


def _mixer_b_kernel(q_ref, k_ref, vt_ref, o_ref, m_ref, acc_ref, *, tkv, nkv, chains):
    m_ref[...] = jnp.full_like(m_ref, NEG_BIG)
    acc_ref[...] = jnp.zeros_like(acc_ref)

    ones = jnp.ones((acc_ref.shape[1] - HEAD_DIM, tkv), BF16)
    tq = q_ref.shape[1]
    cols = m_ref.shape[-1]
    chains_per_head = tq // cols

    def key_block(kb, carry):
        start = pl.multiple_of(kb * tkv, tkv)
        k = k_ref[pl.ds(start, tkv), :]
        vt = jnp.concatenate([vt_ref[:, pl.ds(start, tkv)], ones], axis=0)

        def scores_and_max(c):
            q = q_ref[c // chains_per_head, pl.ds((c % chains_per_head) * cols, cols), :]
            st = lax.dot_general(k, q, (((1,), (1,)), ((), ())), preferred_element_type=F32)
            return st, jnp.maximum(m_ref[c], _reduce_rows(st, jnp.maximum))

        def accumulate(c, st, m_new):
            alpha = jnp.exp2(m_ref[c] - m_new)
            p = jnp.exp2(st - m_new).astype(BF16)
            acc_ref[c] = alpha * acc_ref[c] + jnp.dot(vt, p, preferred_element_type=F32)
            m_ref[c] = m_new

        pending = [scores_and_max(c) for c in range(chains)]
        for c in range(chains):
            accumulate(c, *pending[c])
        return carry

    lax.fori_loop(0, nkv, key_block, 0)

    for c in range(chains):
        head, part = divmod(c, chains_per_head)
        acc = acc_ref[c]
        o_ref[part * cols:(part + 1) * cols, head * HEAD_DIM:(head + 1) * HEAD_DIM] = (
            acc[:HEAD_DIM] / acc[HEAD_DIM:HEAD_DIM + 1]).T


def _mixer_b(q, k, vt, batch, seq):
    rows = batch * seq
    tq = min(1024, seq)
    tkv = min(1024, seq)
    nq, nkv = seq // tq, seq // tkv
    cols = min(MIXER_B_CHAIN_COLS, tq)
    chains = GQA_GROUP * tq // cols
    q_spec = pl.BlockSpec((GQA_GROUP, tq, HEAD_DIM), lambda b, g, qi: (g, b * nq + qi, 0))
    k_spec = pl.BlockSpec((None, seq, HEAD_DIM), lambda b, g, qi: (g, b, 0))
    vt_spec = pl.BlockSpec((None, HEAD_DIM, seq), lambda b, g, qi: (g, 0, b))
    return pl.pallas_call(
        functools.partial(_mixer_b_kernel, tkv=tkv, nkv=nkv, chains=chains),
        out_shape=jax.ShapeDtypeStruct((rows, WIDTH_B), F32),
        grid=(batch, N_KV_B, nq),
        in_specs=[q_spec, k_spec, vt_spec],
        out_specs=pl.BlockSpec((tq, GQA_GROUP * HEAD_DIM), lambda b, g, qi: (b * nq + qi, g)),
        scratch_shapes=[
            pltpu.VMEM((chains, 1, cols), F32),
            pltpu.VMEM((chains, HEAD_DIM + BF16_SUBLANES, cols), F32),
        ],
        compiler_params=_params(("parallel", "parallel", "arbitrary")),
        name="mixer_b_gqa",
    )(q, k, vt)


def _paired_layout(rotary_halves):
    lower = [d for a, _, w in rotary_halves for d in range(a, a + w)]
    upper = [d for _, b, w in rotary_halves for d in range(b, b + w)]
    rest = [d for d in range(HEAD_DIM) if d not in lower and d not in upper]
    lower += rest[:len(rest) // 2]
    upper += rest[len(rest) // 2:]
    return np.array(lower + upper)


PERM_A = _paired_layout([(0, PARTIAL_ROT_DIM // 2, PARTIAL_ROT_DIM // 2)])
PERM_B = _paired_layout([(0, HEAD_DIM // 4, HEAD_DIM // 4), (HEAD_DIM // 2, 3 * HEAD_DIM // 4, HEAD_DIM // 4)])


def _rotary_tables(angle_groups):
    seq = angle_groups[0].shape[0]
    cos = np.ones((seq, HEAD_DIM), np.float64)
    sin = np.zeros((seq, HEAD_DIM), np.float64)
    lane = 0
    for ang in angle_groups:
        w = ang.shape[1]
        for base, sign in ((lane, -1.0), (lane + HEAD_DIM // 2, 1.0)):
            cos[:, base:base + w] = np.cos(ang)
            sin[:, base:base + w] = sign * np.sin(ang)
        lane += w
    return tuple(jnp.asarray(t, F32) for t in (cos, sin))


def _inv_freq(rot_dim, theta):
    return theta ** (-np.arange(0, rot_dim, 2, dtype=np.float64) / rot_dim)


def _partial_rope_tables(seq):
    pos = np.arange(seq, dtype=np.float64)
    return _rotary_tables([pos[:, None] * _inv_freq(PARTIAL_ROT_DIM, ROPE_THETA)[None, :]])


def _axial_rope_tables(seq):
    pos = np.arange(seq)
    inv = _inv_freq(HEAD_DIM // 2, AXIAL_THETA)[None, :]
    row = (pos // GRID_W).astype(np.float64)[:, None]
    col = (pos % GRID_W).astype(np.float64)[:, None]
    return _rotary_tables([row * inv, col * inv])


def _run_group(x, mod, weights, depth):
    batch, seq, d = x.shape
    rows = batch * seq
    alpha = (2 * depth) ** 0.25
    tables_a = _partial_rope_tables(seq)
    tables_b = _axial_rope_tables(seq)
    x = x.reshape(rows, d)

    def chunks(l):
        m = mod[l].reshape(batch, 6, 1, d)
        return [m[:, i] for i in range(6)]

    sh1, sc1, g1, sh2, sc2, g2 = chunks(0)
    h = _modulate(x, sc1, sh1, seq)
    for l in range(depth):
        w = weights[l]
        qkv_a = _gemm_proj_a(h, w["w_in"], tables_a, seq)
        q_b = _gemm_proj_bq(h, w["w_in"], tables_b, w["g_q"], seq)
        k_b, vt_b = _gemm_proj_bkv(h, w["w_in"], tables_b, w["g_k"], seq)
        o_a = _mixer_a(qkv_a, batch, seq)
        o_b = _mixer_b(q_b, k_b, vt_b, batch, seq)
        mixed = _norm_mix(o_a, o_b, w["g_out_a"], w["g_out_b"], seq)
        y = _gemm_resid(mixed, w["w_out"], x, g1, seq, alpha)
        x, h = _layernorm_modulate(y, w["ln1_g"], w["ln1_b"], sc2, sh2, seq)
        f = _gemm_relu2(h, w["w_up"])
        y = _gemm_resid(f, w["w_down"], x, g2, seq, alpha)
        if l + 1 < depth:
            sh1, sc1, g1, sh2, sc2, g2 = chunks(l + 1)
            x, h = _layernorm_modulate(y, w["ln2_g"], w["ln2_b"], sc1, sh1, seq)
        else:
            x = _layernorm(y, w["ln2_g"], w["ln2_b"], seq)
    return x.reshape(batch, seq, d)


def _permute_lanes(y, perm):
    lane = lax.broadcasted_iota(jnp.int32, y.shape, 1)
    shifts = (np.arange(HEAD_DIM) - perm) % HEAD_DIM
    out = y
    for shift in sorted(set(shifts.tolist()) - {0}):
        lanes = np.nonzero(shifts == shift)[0]
        lo, hi = int(lanes[0]), int(lanes[-1])
        assert hi - lo + 1 == lanes.size
        out = jnp.where(jnp.logical_and(lane >= lo, lane <= hi), pltpu.roll(y, shift, 1), out)
    return out


def _prep_w_in_kernel(w_ref, o_ref):
    qk_a_heads = 2 * N_HEADS_A
    b_first = PROJ_A // HEAD_DIM
    qk_b_heads = N_HEADS_B + N_KV_B
    for head in range(o_ref.shape[-1] // HEAD_DIM):
        cols = slice(head * HEAD_DIM, (head + 1) * HEAD_DIM)
        y = w_ref[:, cols]
        if head < qk_a_heads:
            y = _permute_lanes(y, PERM_A)
        elif b_first <= head < b_first + qk_b_heads:
            y = _permute_lanes(y, PERM_B)
        o_ref[:, cols] = y.astype(o_ref.dtype)


def _prep_w_in(w_in):
    depth, d, n = w_in.shape
    tk = min(256, d)
    return pl.pallas_call(
        _prep_w_in_kernel,
        out_shape=jax.ShapeDtypeStruct((depth, d, n), BF16),
        grid=(depth, d // tk),
        in_specs=[pl.BlockSpec((None, tk, n), lambda l, i: (l, i, 0))],
        out_specs=pl.BlockSpec((None, tk, n), lambda l, i: (l, i, 0)),
        compiler_params=_params(("parallel", "parallel")),
        name="prep_w_in",
    )(w_in)


def kernel(x_prompt, x_sample, c_prompt, c_sample, w_ada, b_ada, w_in, g_q, g_k, g_out_a, g_out_b, w_out,
           ln1_g, ln1_b, w_up, w_down, ln2_g, ln2_b):
    depth = w_ada.shape[0]
    n_prompt, n_sample = c_prompt.shape[0], c_sample.shape[0]
    pad = -(n_prompt + n_sample) % 8
    c_rows = jnp.concatenate([c_prompt, c_sample, jnp.zeros((pad, c_prompt.shape[1]), F32)], axis=0)
    mod = _ada(c_rows, w_ada, b_ada)

    w_in = _prep_w_in(w_in)
    g_q, g_k = g_q[:, PERM_B], g_k[:, PERM_B]
    w_out, w_up, w_down = (w.astype(BF16) for w in (w_out, w_up, w_down))
    weights = [
        dict(w_in=(w_in, l), w_out=(w_out, l), w_up=(w_up, l), w_down=(w_down, l),
             g_q=g_q[l], g_k=g_k[l], g_out_a=g_out_a[l], g_out_b=g_out_b[l],
             ln1_g=ln1_g[l], ln1_b=ln1_b[l], ln2_g=ln2_g[l], ln2_b=ln2_b[l])
        for l in range(depth)
    ]
    y_prompt = _run_group(x_prompt, mod[:, :n_prompt], weights, depth)
    y_sample = _run_group(x_sample, mod[:, n_prompt:n_prompt + n_sample], weights, depth)
    return (y_prompt, y_sample)
```

```python
import functools

import numpy as np
import jax
import jax.numpy as jnp
from jax import lax
from jax.experimental import pallas as pl
from jax.experimental.pallas import tpu as pltpu

HEAD_DIM = 128
N_HEADS_A = 16
N_HEADS_B = 16
N_KV_B = 4
GQA_GROUP = N_HEADS_B // N_KV_B
WIDTH_A = N_HEADS_A * HEAD_DIM
WIDTH_B = N_HEADS_B * HEAD_DIM
KV_WIDTH_B = N_KV_B * HEAD_DIM
PROJ_A = 3 * WIDTH_A
PROJ_B = WIDTH_B + 2 * KV_WIDTH_B
DILATIONS = (1, 4, 16)
HALF_WINDOW = 64
PARTIAL_ROT_DIM = HEAD_DIM // 4
ROPE_THETA = 500000.0
AXIAL_THETA = 10000.0
GRID_W = 64
LN_EPS = 1e-5
RMS_EPS = 1e-6
NEG_BIG = -1e30
ATTN_SCALE = HEAD_DIM ** -0.5
LOG2_E = 1.4426950408889634

V7X_VMEM_BYTES = 64 * 1024 * 1024
VMEM_LIMIT_BYTES = V7X_VMEM_BYTES - 6 * 1024 * 1024
LANES = 128

F32 = jnp.float32
BF16 = jnp.bfloat16


def _params(semantics):
    return pltpu.CompilerParams(dimension_semantics=semantics, vmem_limit_bytes=VMEM_LIMIT_BYTES)


def _ada_kernel(c_ref, w_ref, b_ref, o_ref):
    c = c_ref[...]
    s = (c * (1.0 / (1.0 + jnp.exp(-c)))).astype(BF16)
    w = w_ref[...].astype(BF16)
    o_ref[...] = jnp.dot(s, w, preferred_element_type=F32) + b_ref[...]


def _ada(c_rows, w_ada, b_ada):
    depth, d, n = w_ada.shape
    rows = c_rows.shape[0]
    tn = min(512, n)
    return pl.pallas_call(
        _ada_kernel,
        out_shape=jax.ShapeDtypeStruct((depth, rows, n), F32),
        grid=(depth, n // tn),
        in_specs=[
            pl.BlockSpec((rows, d), lambda l, j: (0, 0)),
            pl.BlockSpec((None, d, tn), lambda l, j: (l, 0, j)),
            pl.BlockSpec((None, 1, tn), lambda l, j: (l, 0, j)),
        ],
        out_specs=pl.BlockSpec((None, rows, tn), lambda l, j: (l, 0, j)),
        compiler_params=_params(("parallel", "parallel")),
        name="ada_modulation",
    )(c_rows, w_ada, b_ada.reshape(depth, 1, n))


def _modulate_kernel(x_ref, sc_ref, sh_ref, h_ref):
    h_ref[...] = (x_ref[...] * (1.0 + sc_ref[...]) + sh_ref[...]).astype(BF16)


def _layernorm_rows(y, g, b):
    mu = jnp.mean(y, axis=-1, keepdims=True)
    yc = y - mu
    var = jnp.mean(yc * yc, axis=-1, keepdims=True)
    return yc * lax.rsqrt(var + LN_EPS) * g + b


def _ln_kernel(y_ref, g_ref, b_ref, x_ref):
    x_ref[...] = _layernorm_rows(y_ref[...], g_ref[...], b_ref[...])


def _ln_modulate_kernel(y_ref, g_ref, b_ref, sc_ref, sh_ref, x_ref, h_ref):
    x = _layernorm_rows(y_ref[...], g_ref[...], b_ref[...])
    x_ref[...] = x
    h_ref[...] = (x * (1.0 + sc_ref[...]) + sh_ref[...]).astype(BF16)


def _rms_rows(o, g):
    return o * lax.rsqrt(jnp.mean(o * o, axis=-1, keepdims=True) + RMS_EPS) * g


def _norm_mix_kernel(oa_ref, ob_ref, ga_ref, gb_ref, m_ref):
    wa = oa_ref.shape[-1]
    m_ref[:, :wa] = _rms_rows(oa_ref[...], ga_ref[...]).astype(BF16)
    m_ref[:, wa:] = _rms_rows(ob_ref[...], gb_ref[...]).astype(BF16)


def _row_tile(rows, seq):
    return min(512, seq, rows)


def _modulate(x, sc, sh, seq):
    rows, d = x.shape
    tm = _row_tile(rows, seq)
    per_batch = lambda i: ((i * tm) // seq, 0, 0)
    return pl.pallas_call(
        _modulate_kernel,
        out_shape=jax.ShapeDtypeStruct((rows, d), BF16),
        grid=(rows // tm,),
        in_specs=[
            pl.BlockSpec((tm, d), lambda i: (i, 0)),
            pl.BlockSpec((None, 1, d), per_batch),
            pl.BlockSpec((None, 1, d), per_batch),
        ],
        out_specs=pl.BlockSpec((tm, d), lambda i: (i, 0)),
        compiler_params=_params(("parallel",)),
        name="modulate",
    )(x, sc, sh)


def _layernorm(y, g, b, seq):
    rows, d = y.shape
    tm = _row_tile(rows, seq)
    return pl.pallas_call(
        _ln_kernel,
        out_shape=jax.ShapeDtypeStruct((rows, d), F32),
        grid=(rows // tm,),
        in_specs=[
            pl.BlockSpec((tm, d), lambda i: (i, 0)),
            pl.BlockSpec((1, d), lambda i: (0, 0)),
            pl.BlockSpec((1, d), lambda i: (0, 0)),
        ],
        out_specs=pl.BlockSpec((tm, d), lambda i: (i, 0)),
        compiler_params=_params(("parallel",)),
        name="layernorm",
    )(y, g.reshape(1, d), b.reshape(1, d))


def _layernorm_modulate(y, g, b, sc, sh, seq):
    rows, d = y.shape
    tm = _row_tile(rows, seq)
    per_batch = lambda i: ((i * tm) // seq, 0, 0)
    return pl.pallas_call(
        _ln_modulate_kernel,
        out_shape=(jax.ShapeDtypeStruct((rows, d), F32), jax.ShapeDtypeStruct((rows, d), BF16)),
        grid=(rows // tm,),
        in_specs=[
            pl.BlockSpec((tm, d), lambda i: (i, 0)),
            pl.BlockSpec((1, d), lambda i: (0, 0)),
            pl.BlockSpec((1, d), lambda i: (0, 0)),
            pl.BlockSpec((None, 1, d), per_batch),
            pl.BlockSpec((None, 1, d), per_batch),
        ],
        out_specs=(pl.BlockSpec((tm, d), lambda i: (i, 0)), pl.BlockSpec((tm, d), lambda i: (i, 0))),
        compiler_params=_params(("parallel",)),
        name="layernorm_modulate",
    )(y, g.reshape(1, d), b.reshape(1, d), sc, sh)


def _norm_mix(o_a, o_b, g_a, g_b, seq):
    rows, wa = o_a.shape
    wb = o_b.shape[1]
    tm = _row_tile(rows, seq)
    return pl.pallas_call(
        _norm_mix_kernel,
        out_shape=jax.ShapeDtypeStruct((rows, wa + wb), BF16),
        grid=(rows // tm,),
        in_specs=[
            pl.BlockSpec((tm, wa), lambda i: (i, 0)),
            pl.BlockSpec((tm, wb), lambda i: (i, 0)),
            pl.BlockSpec((1, wa), lambda i: (0, 0)),
            pl.BlockSpec((1, wb), lambda i: (0, 0)),
        ],
        out_specs=pl.BlockSpec((tm, wa + wb), lambda i: (i, 0)),
        compiler_params=_params(("parallel",)),
        name="norm_mix",
    )(o_a, o_b, g_a.reshape(1, wa), g_b.reshape(1, wb))


def _accumulate(a_ref, w_ref, acc_ref, nk, epilogue):
    def part():
        return jnp.dot(a_ref[...], w_ref[...], preferred_element_type=F32)

    if nk == 1:
        epilogue(part())
        return
    k = pl.program_id(2)

    @pl.when(k == 0)
    def _():
        acc_ref[...] = part()

    @pl.when(jnp.logical_and(k > 0, k < nk - 1))
    def _():
        acc_ref[...] += part()

    @pl.when(k == nk - 1)
    def _():
        epilogue(acc_ref[...] + part())


def _rotate_pairs(y, cos, sin):
    return y * cos + pltpu.roll(y, HEAD_DIM // 2, 1) * sin


def _gemm_relu2_kernel(a_ref, w_ref, o_ref, *scratch, nk):
    def epilogue(acc):
        o_ref[...] = jnp.square(jnp.maximum(acc, 0.0)).astype(o_ref.dtype)
    _accumulate(a_ref, w_ref, scratch[0] if scratch else None, nk, epilogue)


def _gemm_resid_kernel(a_ref, w_ref, x_ref, gate_ref, o_ref, *scratch, nk, alpha):
    def epilogue(acc):
        o_ref[...] = alpha * x_ref[...] + gate_ref[...] * acc
    _accumulate(a_ref, w_ref, scratch[0] if scratch else None, nk, epilogue)


def _gemm_proj_a_kernel(a_ref, w_ref, cos_ref, sin_ref, o_ref, *scratch, nk, rope_tiles):
    heads = o_ref.shape[0]

    def epilogue(acc):
        j = pl.program_id(1)

        @pl.when(j < rope_tiles)
        def _():
            for hh in range(heads):
                y = acc[:, hh * HEAD_DIM:(hh + 1) * HEAD_DIM]
                o_ref[hh] = _rotate_pairs(y, cos_ref[...], sin_ref[...])

        @pl.when(j >= rope_tiles)
        def _():
            for hh in range(heads):
                o_ref[hh] = acc[:, hh * HEAD_DIM:(hh + 1) * HEAD_DIM]

    _accumulate(a_ref, w_ref, scratch[0] if scratch else None, nk, epilogue)


def _axial_normed_rotated(y, g_ref, cos_ref, sin_ref):
    return _rotate_pairs(_rms_rows(y, g_ref[...]), cos_ref[...], sin_ref[...])


def _gemm_proj_bq_kernel(a_ref, w_ref, cos_ref, sin_ref, gq_ref, o_ref, *scratch, nk):
    def epilogue(acc):
        for hh in range(o_ref.shape[0]):
            y = acc[:, hh * HEAD_DIM:(hh + 1) * HEAD_DIM]
            q = _axial_normed_rotated(y, gq_ref, cos_ref, sin_ref)
            o_ref[hh] = (q * (ATTN_SCALE * LOG2_E)).astype(o_ref.dtype)

    _accumulate(a_ref, w_ref, scratch[0] if scratch else None, nk, epilogue)


def _gemm_proj_bkv_kernel(a_ref, w_ref, cos_ref, sin_ref, gk_ref, k_ref, vt_ref, *scratch, nk):
    def epilogue(acc):
        for hh in range(N_KV_B):
            y = acc[:, hh * HEAD_DIM:(hh + 1) * HEAD_DIM]
            k_ref[hh] = _axial_normed_rotated(y, gk_ref, cos_ref, sin_ref).astype(k_ref.dtype)
        for hh in range(N_KV_B):
            y = acc[:, (N_KV_B + hh) * HEAD_DIM:(N_KV_B + hh + 1) * HEAD_DIM]
            vt_ref[hh] = y.T.astype(vt_ref.dtype)

    _accumulate(a_ref, w_ref, scratch[0] if scratch else None, nk, epilogue)


def _gemm_tiles(m, n, k, seq):
    tm = min(1024, seq, m)
    tn = min(1024, n)
    tk = k if k <= 4096 else 2048
    return tm, tn, tk


def _gemm_call(kernel_fn, a, w, extra_inputs, extra_specs, out_shape, out_spec, tiles, n_cols, col_offset, name):
    w_all, layer = w
    m, k = a.shape
    tm, tn, tk = tiles
    nk = k // tk
    joff = col_offset // tn
    scratch = [pltpu.VMEM((tm, tn), F32)] if nk > 1 else []
    return pl.pallas_call(
        functools.partial(kernel_fn, nk=nk),
        out_shape=out_shape,
        grid=(m // tm, n_cols // tn, nk),
        in_specs=[
            pl.BlockSpec((tm, tk), lambda i, j, kk: (i, kk)),
            pl.BlockSpec((None, tk, tn), lambda i, j, kk: (layer, kk, j + joff)),
        ] + extra_specs,
        out_specs=out_spec,
        scratch_shapes=scratch,
        compiler_params=_params(("parallel", "parallel", "arbitrary")),
        name=name,
    )(a, w_all, *extra_inputs)


def _gemm_relu2(a, w):
    m, k = a.shape
    n = w[0].shape[2]
    tiles = _gemm_tiles(m, n, k, m)
    tm, tn, _ = tiles
    return _gemm_call(
        _gemm_relu2_kernel, a, w, [], [],
        jax.ShapeDtypeStruct((m, n), BF16),
        pl.BlockSpec((tm, tn), lambda i, j, kk: (i, j)),
        tiles, n, 0, "gemm_relu2")


def _gemm_resid(a, w, x, gate, seq, alpha):
    m, k = a.shape
    n = w[0].shape[2]
    tiles = _gemm_tiles(m, n, k, seq)
    tm, tn, _ = tiles
    return _gemm_call(
        functools.partial(_gemm_resid_kernel, alpha=alpha), a, w, [x, gate],
        [pl.BlockSpec((tm, tn), lambda i, j, kk: (i, j)),
         pl.BlockSpec((None, 1, tn), lambda i, j, kk: ((i * tm) // seq, 0, j))],
        jax.ShapeDtypeStruct((m, n), F32),
        pl.BlockSpec((tm, tn), lambda i, j, kk: (i, j)),
        tiles, n, 0, "gemm_resid")


def _rope_specs(tm, seq):
    nrow = seq // tm
    return [pl.BlockSpec((tm, LANES), lambda i, j, kk: (i % nrow, 0))] * 2


def _gemm_proj_a(h, w_in, tables, seq):
    m, k = h.shape
    tiles = _gemm_tiles(m, PROJ_A, k, seq)
    tm, tn, _ = tiles
    heads = tn // HEAD_DIM
    return _gemm_call(
        functools.partial(_gemm_proj_a_kernel, rope_tiles=2 * WIDTH_A // tn), h, w_in, list(tables),
        _rope_specs(tm, seq),
        jax.ShapeDtypeStruct((PROJ_A // HEAD_DIM, m, HEAD_DIM), F32),
        pl.BlockSpec((heads, tm, HEAD_DIM), lambda i, j, kk: (j, i, 0)),
        tiles, PROJ_A, 0, "gemm_proj_a")


def _gemm_proj_bq(h, w_in, tables, g_q, seq):
    m, k = h.shape
    tiles = _gemm_tiles(m, WIDTH_B, k, seq)
    tm, tn, _ = tiles
    heads = tn // HEAD_DIM
    return _gemm_call(
        _gemm_proj_bq_kernel, h, w_in, list(tables) + [g_q.reshape(1, HEAD_DIM)],
        _rope_specs(tm, seq) + [pl.BlockSpec((1, HEAD_DIM), lambda i, j, kk: (0, 0))],
        jax.ShapeDtypeStruct((N_HEADS_B, m, HEAD_DIM), BF16),
        pl.BlockSpec((heads, tm, HEAD_DIM), lambda i, j, kk: (j, i, 0)),
        tiles, WIDTH_B, PROJ_A, "gemm_proj_bq")


def _gemm_proj_bkv(h, w_in, tables, g_k, seq):
    m, k = h.shape
    tm, _, tk = _gemm_tiles(m, 2 * KV_WIDTH_B, k, seq)
    tiles = (tm, 2 * KV_WIDTH_B, tk)
    return _gemm_call(
        _gemm_proj_bkv_kernel, h, w_in, list(tables) + [g_k.reshape(1, HEAD_DIM)],
        _rope_specs(tm, seq) + [pl.BlockSpec((1, HEAD_DIM), lambda i, j, kk: (0, 0))],
        (jax.ShapeDtypeStruct((N_KV_B, m, HEAD_DIM), BF16), jax.ShapeDtypeStruct((N_KV_B, HEAD_DIM, m), BF16)),
        (pl.BlockSpec((N_KV_B, tm, HEAD_DIM), lambda i, j, kk: (0, i, 0)),
         pl.BlockSpec((N_KV_B, HEAD_DIM, tm), lambda i, j, kk: (0, 0, i))),
        tiles, 2 * KV_WIDTH_B, PROJ_A + WIDTH_B, "gemm_proj_bkv")


def _mixer_a_kernel(q_ref, k_ref, v_ref, o_ref, acc_ref, m_ref, d_ref, *, seq, window, plan):
    win = pl.program_id(2)

    for branch, (dil, tq) in enumerate(plan):
        length = seq // dil
        tkv = tq + 2 * HALF_WINDOW
        blocks = window // (dil * tq)
        offset = (lax.broadcasted_iota(jnp.int32, (tq, tkv), 1)
                  - lax.broadcasted_iota(jnp.int32, (tq, tkv), 0))

        def rows(start, size, dil=dil):
            return pl.ds(start, size) if dil == 1 else pl.ds(start, size, stride=dil)

        for res in range(dil):
            for blk in range(blocks):
                loc = res + dil * tq * blk
                q0 = win * (window // dil) + tq * blk
                k0 = jnp.clip(q0 - HALF_WINDOW, 0, length - tkv)
                q = (q_ref[rows(loc, tq), :] * (ATTN_SCALE * LOG2_E)).astype(BF16)
                k = k_ref[rows(res + dil * k0, tkv), :].astype(BF16)
                v = v_ref[rows(res + dil * k0, tkv), :].astype(BF16)
                s = lax.dot_general(q, k, (((1,), (1,)), ((), ())), preferred_element_type=F32)
                shifted = offset + (k0 - q0 + HALF_WINDOW)
                valid = lax.bitcast_convert_type(shifted, jnp.uint32) <= 2 * HALF_WINDOW
                s = jnp.where(valid, s, NEG_BIG)
                m = jnp.max(s, axis=-1, keepdims=True)
                p = jnp.exp2(s - m).astype(BF16)
                v_ones = jnp.concatenate([v, jnp.ones((tkv, LANES), BF16)], axis=1)
                pv = jnp.dot(p, v_ones, preferred_element_type=F32)
                acc_ref[branch, rows(loc, tq), :] = pv[:, :HEAD_DIM]
                m_ref[branch, rows(loc, tq), :] = jnp.broadcast_to(m, (tq, LANES))
                d_ref[branch, rows(loc, tq), :] = pv[:, HEAD_DIM:]

    m_all = m_ref[0]
    for branch in range(1, len(plan)):
        m_all = jnp.maximum(m_all, m_ref[branch])
    num = jnp.zeros(o_ref.shape, F32)
    tot = jnp.zeros(o_ref.shape, F32)
    for branch in range(len(plan)):
        w = jnp.exp2(m_ref[branch] - m_all)
        num = num + w * acc_ref[branch]
        tot = tot + w * d_ref[branch]
    o_ref[...] = num / tot


def _mixer_a_plan(seq, window):
    plan = []
    for dil in DILATIONS:
        length = seq // dil
        tq = min(128, length // 2, window // dil)
        assert length >= tq + 2 * HALF_WINDOW and window % (dil * tq) == 0, (seq, window, dil)
        plan.append((dil, tq))
    return tuple(plan)


def _mixer_a(qkv, batch, seq):
    rows = batch * seq
    window = min(2048, seq)
    nwin = seq // window
    plan = _mixer_a_plan(seq, window)
    q_spec = pl.BlockSpec((None, window, HEAD_DIM), lambda b, h, w: (h, b * nwin + w, 0))
    k_spec = pl.BlockSpec((None, seq, HEAD_DIM), lambda b, h, w: (N_HEADS_A + h, b, 0))
    v_spec = pl.BlockSpec((None, seq, HEAD_DIM), lambda b, h, w: (2 * N_HEADS_A + h, b, 0))
    stat = pltpu.VMEM((len(plan), window, LANES), F32)
    return pl.pallas_call(
        functools.partial(_mixer_a_kernel, seq=seq, window=window, plan=plan),
        out_shape=jax.ShapeDtypeStruct((rows, WIDTH_A), F32),
        grid=(batch, N_HEADS_A, nwin),
        in_specs=[q_spec, k_spec, v_spec],
        out_specs=pl.BlockSpec((window, HEAD_DIM), lambda b, h, w: (b * nwin + w, h)),
        scratch_shapes=[stat, stat, stat],
        compiler_params=_params(("parallel", "parallel", "arbitrary")),
        name="mixer_a_dilated",
    )(qkv, qkv, qkv)


SUBLANES = 8
BF16_SUBLANES = 16
MIXER_B_CHAIN_COLS = 256


def _reduce_rows(x, op):
    parts = [x[r:r + SUBLANES] for r in range(0, x.shape[0], SUBLANES)]
    while len(parts) > 1:
        parts = [op(parts[i], parts[i + 1]) for i in range(0, len(parts) - 1, 2)] + (
            [parts[-1]] if len(parts) % 2 else [])
    top = parts[0]
    rows = [top[r:r + 1] for r in range(top.shape[0])]
    while len(rows) > 1:
        rows = [op(rows[i], rows[i + 1]) for i in range(0, len(rows), 2)]
    return rows[0]


def _mixer_b_kernel(q_ref, k_ref, vt_ref, o_ref, m_ref, acc_ref, *, nkv, chains):
    ki = pl.program_id(3)

    @pl.when(ki == 0)
    def _():
        m_ref[...] = jnp.full_like(m_ref, NEG_BIG)
        acc_ref[...] = jnp.zeros_like(acc_ref)

    k = k_ref[...]
    ones = jnp.ones((acc_ref.shape[1] - HEAD_DIM, k.shape[0]), BF16)
    vt = jnp.concatenate([vt_ref[...], ones], axis=0)
    tq = q_ref.shape[1]
    cols = m_ref.shape[-1]
    chains_per_head = tq // cols

    def scores_and_max(c):
        q = q_ref[c // chains_per_head, pl.ds((c % chains_per_head) * cols, cols), :]
        st = lax.dot_general(k, q, (((1,), (1,)), ((), ())), preferred_element_type=F32)
        return st, jnp.maximum(m_ref[c], _reduce_rows(st, jnp.maximum))

    def accumulate(c, st, m_new):
        alpha = jnp.exp2(m_ref[c] - m_new)
        p = jnp.exp2(st - m_new).astype(BF16)
        acc_ref[c] = alpha * acc_ref[c] + jnp.dot(vt, p, preferred_element_type=F32)
        m_ref[c] = m_new

    pending = [scores_and_max(c) for c in range(chains)]
    for c in range(chains):
        accumulate(c, *pending[c])

    @pl.when(ki == nkv - 1)
    def _():
        for c in range(chains):
            head, part = divmod(c, chains_per_head)
            acc = acc_ref[c]
            o_ref[part * cols:(part + 1) * cols, head * HEAD_DIM:(head + 1) * HEAD_DIM] = (
                acc[:HEAD_DIM] / acc[HEAD_DIM:HEAD_DIM + 1]).T


def _mixer_b(q, k, vt, batch, seq):
    rows = batch * seq
    tq = min(1024, seq)
    tkv = min(2048, seq)
    nq, nkv = seq // tq, seq // tkv
    cols = min(MIXER_B_CHAIN_COLS, tq)
    chains = GQA_GROUP * tq // cols
    q_spec = pl.BlockSpec((GQA_GROUP, tq, HEAD_DIM), lambda b, g, qi, ki: (g, b * nq + qi, 0))
    k_spec = pl.BlockSpec((None, tkv, HEAD_DIM), lambda b, g, qi, ki: (g, b * nkv + ki, 0))
    vt_spec = pl.BlockSpec((None, HEAD_DIM, tkv), lambda b, g, qi, ki: (g, 0, b * nkv + ki))
    return pl.pallas_call(
        functools.partial(_mixer_b_kernel, nkv=nkv, chains=chains),
        out_shape=jax.ShapeDtypeStruct((rows, WIDTH_B), F32),
        grid=(batch, N_KV_B, nq, nkv),
        in_specs=[q_spec, k_spec, vt_spec],
        out_specs=pl.BlockSpec((tq, GQA_GROUP * HEAD_DIM), lambda b, g, qi, ki: (b * nq + qi, g)),
        scratch_shapes=[
            pltpu.VMEM((chains, 1, cols), F32),
            pltpu.VMEM((chains, HEAD_DIM + BF16_SUBLANES, cols), F32),
        ],
        compiler_params=_params(("parallel", "parallel", "parallel", "arbitrary")),
        name="mixer_b_gqa",
    )(q, k, vt)


def _paired_layout(rotary_halves):
    lower = [d for a, _, w in rotary_halves for d in range(a, a + w)]
    upper = [d for _, b, w in rotary_halves for d in range(b, b + w)]
    rest = [d for d in range(HEAD_DIM) if d not in lower and d not in upper]
    lower += rest[:len(rest) // 2]
    upper += rest[len(rest) // 2:]
    return np.array(lower + upper)


PERM_A = _paired_layout([(0, PARTIAL_ROT_DIM // 2, PARTIAL_ROT_DIM // 2)])
PERM_B = _paired_layout([(0, HEAD_DIM // 4, HEAD_DIM // 4), (HEAD_DIM // 2, 3 * HEAD_DIM // 4, HEAD_DIM // 4)])


def _rotary_tables(angle_groups):
    seq = angle_groups[0].shape[0]
    cos = np.ones((seq, HEAD_DIM), np.float64)
    sin = np.zeros((seq, HEAD_DIM), np.float64)
    lane = 0
    for ang in angle_groups:
        w = ang.shape[1]
        for base, sign in ((lane, -1.0), (lane + HEAD_DIM // 2, 1.0)):
            cos[:, base:base + w] = np.cos(ang)
            sin[:, base:base + w] = sign * np.sin(ang)
        lane += w
    return tuple(jnp.asarray(t, F32) for t in (cos, sin))


def _inv_freq(rot_dim, theta):
    return theta ** (-np.arange(0, rot_dim, 2, dtype=np.float64) / rot_dim)


def _partial_rope_tables(seq):
    pos = np.arange(seq, dtype=np.float64)
    return _rotary_tables([pos[:, None] * _inv_freq(PARTIAL_ROT_DIM, ROPE_THETA)[None, :]])


def _axial_rope_tables(seq):
    pos = np.arange(seq)
    inv = _inv_freq(HEAD_DIM // 2, AXIAL_THETA)[None, :]
    row = (pos // GRID_W).astype(np.float64)[:, None]
    col = (pos % GRID_W).astype(np.float64)[:, None]
    return _rotary_tables([row * inv, col * inv])


def _run_group(x, mod, weights, depth):
    batch, seq, d = x.shape
    rows = batch * seq
    alpha = (2 * depth) ** 0.25
    tables_a = _partial_rope_tables(seq)
    tables_b = _axial_rope_tables(seq)
    x = x.reshape(rows, d)

    def chunks(l):
        m = mod[l].reshape(batch, 6, 1, d)
        return [m[:, i] for i in range(6)]

    sh1, sc1, g1, sh2, sc2, g2 = chunks(0)
    h = _modulate(x, sc1, sh1, seq)
    for l in range(depth):
        w = weights[l]
        qkv_a = _gemm_proj_a(h, w["w_in"], tables_a, seq)
        q_b = _gemm_proj_bq(h, w["w_in"], tables_b, w["g_q"], seq)
        k_b, vt_b = _gemm_proj_bkv(h, w["w_in"], tables_b, w["g_k"], seq)
        o_a = _mixer_a(qkv_a, batch, seq)
        o_b = _mixer_b(q_b, k_b, vt_b, batch, seq)
        mixed = _norm_mix(o_a, o_b, w["g_out_a"], w["g_out_b"], seq)
        y = _gemm_resid(mixed, w["w_out"], x, g1, seq, alpha)
        x, h = _layernorm_modulate(y, w["ln1_g"], w["ln1_b"], sc2, sh2, seq)
        f = _gemm_relu2(h, w["w_up"])
        y = _gemm_resid(f, w["w_down"], x, g2, seq, alpha)
        if l + 1 < depth:
            sh1, sc1, g1, sh2, sc2, g2 = chunks(l + 1)
            x, h = _layernorm_modulate(y, w["ln2_g"], w["ln2_b"], sc1, sh1, seq)
        else:
            x = _layernorm(y, w["ln2_g"], w["ln2_b"], seq)
    return x.reshape(batch, seq, d)


def _permute_lanes(y, perm):
    lane = lax.broadcasted_iota(jnp.int32, y.shape, 1)
    shifts = (np.arange(HEAD_DIM) - perm) % HEAD_DIM
    out = y
    for shift in sorted(set(shifts.tolist()) - {0}):
        lanes = np.nonzero(shifts == shift)[0]
        lo, hi = int(lanes[0]), int(lanes[-1])
        assert hi - lo + 1 == lanes.size
        out = jnp.where(jnp.logical_and(lane >= lo, lane <= hi), pltpu.roll(y, shift, 1), out)
    return out


def _prep_w_in_kernel(w_ref, o_ref):
    qk_a_heads = 2 * N_HEADS_A
    b_first = PROJ_A // HEAD_DIM
    qk_b_heads = N_HEADS_B + N_KV_B
    for head in range(o_ref.shape[-1] // HEAD_DIM):
        cols = slice(head * HEAD_DIM, (head + 1) * HEAD_DIM)
        y = w_ref[:, cols]
        if head < qk_a_heads:
            y = _permute_lanes(y, PERM_A)
        elif b_first <= head < b_first + qk_b_heads:
            y = _permute_lanes(y, PERM_B)
        o_ref[:, cols] = y.astype(o_ref.dtype)


def _prep_w_in(w_in):
    depth, d, n = w_in.shape
    tk = min(256, d)
    return pl.pallas_call(
        _prep_w_in_kernel,
        out_shape=jax.ShapeDtypeStruct((depth, d, n), BF16),
        grid=(depth, d // tk),
        in_specs=[pl.BlockSpec((None, tk, n), lambda l, i: (l, i, 0))],
        out_specs=pl.BlockSpec((None, tk, n), lambda l, i: (l, i, 0)),
        compiler_params=_params(("parallel", "parallel")),
        name="prep_w_in",
    )(w_in)


def kernel(x_prompt, x_sample, c_prompt, c_sample, w_ada, b_ada, w_in, g_q, g_k, g_out_a, g_out_b, w_out,
           ln1_g, ln1_b, w_up, w_down, ln2_g, ln2_b):
    depth = w_ada.shape[0]
    n_prompt, n_sample = c_prompt.shape[0], c_sample.shape[0]
    pad = -(n_prompt + n_sample) % 8
    c_rows = jnp.concatenate([c_prompt, c_sample, jnp.zeros((pad, c_prompt.shape[1]), F32)], axis=0)
    mod = _ada(c_rows, w_ada, b_ada)

    w_in = _prep_w_in(w_in)
    g_q, g_k = g_q[:, PERM_B], g_k[:, PERM_B]
    w_out, w_up, w_down = (w.astype(BF16) for w in (w_out, w_up, w_down))
    weights = [
        dict(w_in=(w_in, l), w_out=(w_out, l), w_up=(w_up, l), w_down=(w_down, l),
             g_q=g_q[l], g_k=g_k[l], g_out_a=g_out_a[l], g_out_b=g_out_b[l],
             ln1_g=ln1_g[l], ln1_b=ln1_b[l], ln2_g=ln2_g[l], ln2_b=ln2_b[l])
        for l in range(depth)
    ]
    y_prompt = _run_group(x_prompt, mod[:, :n_prompt], weights, depth)
    y_sample = _run_group(x_sample, mod[:, n_prompt:n_prompt + n_sample], weights, depth)
    return (y_prompt, y_sample)
```

```python
import functools

import numpy as np
import jax
import jax.numpy as jnp
from jax import lax
from jax.experimental import pallas as pl
from jax.experimental.pallas import tpu as pltpu

HEAD_DIM = 128
N_HEADS_A = 16
N_HEADS_B = 16
N_KV_B = 4
GQA_GROUP = N_HEADS_B // N_KV_B
WIDTH_A = N_HEADS_A * HEAD_DIM
WIDTH_B = N_HEADS_B * HEAD_DIM
KV_WIDTH_B = N_KV_B * HEAD_DIM
PROJ_A = 3 * WIDTH_A
PROJ_B = WIDTH_B + 2 * KV_WIDTH_B
DILATIONS = (1, 4, 16)
HALF_WINDOW = 64
PARTIAL_ROT_DIM = HEAD_DIM // 4
ROPE_THETA = 500000.0
AXIAL_THETA = 10000.0
GRID_W = 64
LN_EPS = 1e-5
RMS_EPS = 1e-6
NEG_BIG = -1e30
ATTN_SCALE = HEAD_DIM ** -0.5
LOG2_E = 1.4426950408889634

V7X_VMEM_BYTES = 64 * 1024 * 1024
VMEM_LIMIT_BYTES = V7X_VMEM_BYTES - 6 * 1024 * 1024
LANES = 128

F32 = jnp.float32
BF16 = jnp.bfloat16


def _params(semantics):
    return pltpu.CompilerParams(dimension_semantics=semantics, vmem_limit_bytes=VMEM_LIMIT_BYTES)


def _ada_kernel(c_ref, w_ref, b_ref, o_ref):
    c = c_ref[...]
    s = (c * (1.0 / (1.0 + jnp.exp(-c)))).astype(BF16)
    w = w_ref[...].astype(BF16)
    o_ref[...] = jnp.dot(s, w, preferred_element_type=F32) + b_ref[...]


def _ada(c_rows, w_ada, b_ada):
    depth, d, n = w_ada.shape
    rows = c_rows.shape[0]
    tn = min(512, n)
    return pl.pallas_call(
        _ada_kernel,
        out_shape=jax.ShapeDtypeStruct((depth, rows, n), F32),
        grid=(depth, n // tn),
        in_specs=[
            pl.BlockSpec((rows, d), lambda l, j: (0, 0)),
            pl.BlockSpec((None, d, tn), lambda l, j: (l, 0, j)),
            pl.BlockSpec((None, 1, tn), lambda l, j: (l, 0, j)),
        ],
        out_specs=pl.BlockSpec((None, rows, tn), lambda l, j: (l, 0, j)),
        compiler_params=_params(("parallel", "parallel")),
        name="ada_modulation",
    )(c_rows, w_ada, b_ada.reshape(depth, 1, n))


def _modulate_kernel(x_ref, sc_ref, sh_ref, h_ref):
    h_ref[...] = (x_ref[...] * (1.0 + sc_ref[...]) + sh_ref[...]).astype(BF16)


def _layernorm_rows(y, g, b):
    mu = jnp.mean(y, axis=-1, keepdims=True)
    yc = y - mu
    var = jnp.mean(yc * yc, axis=-1, keepdims=True)
    return yc * lax.rsqrt(var + LN_EPS) * g + b


def _ln_kernel(y_ref, g_ref, b_ref, x_ref):
    x_ref[...] = _layernorm_rows(y_ref[...], g_ref[...], b_ref[...])


def _ln_modulate_kernel(y_ref, g_ref, b_ref, sc_ref, sh_ref, x_ref, h_ref):
    x = _layernorm_rows(y_ref[...], g_ref[...], b_ref[...])
    x_ref[...] = x
    h_ref[...] = (x * (1.0 + sc_ref[...]) + sh_ref[...]).astype(BF16)


def _rms_rows(o, g):
    return o * lax.rsqrt(jnp.mean(o * o, axis=-1, keepdims=True) + RMS_EPS) * g


def _norm_mix_kernel(oa_ref, ob_ref, ga_ref, gb_ref, m_ref):
    wa = oa_ref.shape[-1]
    m_ref[:, :wa] = _rms_rows(oa_ref[...], ga_ref[...]).astype(BF16)
    m_ref[:, wa:] = _rms_rows(ob_ref[...], gb_ref[...]).astype(BF16)


def _row_tile(rows, seq):
    return min(256, seq, rows)


def _modulate(x, sc, sh, seq):
    rows, d = x.shape
    tm = _row_tile(rows, seq)
    per_batch = lambda i: ((i * tm) // seq, 0, 0)
    return pl.pallas_call(
        _modulate_kernel,
        out_shape=jax.ShapeDtypeStruct((rows, d), BF16),
        grid=(rows // tm,),
        in_specs=[
            pl.BlockSpec((tm, d), lambda i: (i, 0)),
            pl.BlockSpec((None, 1, d), per_batch),
            pl.BlockSpec((None, 1, d), per_batch),
        ],
        out_specs=pl.BlockSpec((tm, d), lambda i: (i, 0)),
        compiler_params=_params(("parallel",)),
        name="modulate",
    )(x, sc, sh)


def _layernorm(y, g, b, seq):
    rows, d = y.shape
    tm = _row_tile(rows, seq)
    return pl.pallas_call(
        _ln_kernel,
        out_shape=jax.ShapeDtypeStruct((rows, d), F32),
        grid=(rows // tm,),
        in_specs=[
            pl.BlockSpec((tm, d), lambda i: (i, 0)),
            pl.BlockSpec((1, d), lambda i: (0, 0)),
            pl.BlockSpec((1, d), lambda i: (0, 0)),
        ],
        out_specs=pl.BlockSpec((tm, d), lambda i: (i, 0)),
        compiler_params=_params(("parallel",)),
        name="layernorm",
    )(y, g.reshape(1, d), b.reshape(1, d))


def _layernorm_modulate(y, g, b, sc, sh, seq):
    rows, d = y.shape
    tm = _row_tile(rows, seq)
    per_batch = lambda i: ((i * tm) // seq, 0, 0)
    return pl.pallas_call(
        _ln_modulate_kernel,
        out_shape=(jax.ShapeDtypeStruct((rows, d), F32), jax.ShapeDtypeStruct((rows, d), BF16)),
        grid=(rows // tm,),
        in_specs=[
            pl.BlockSpec((tm, d), lambda i: (i, 0)),
            pl.BlockSpec((1, d), lambda i: (0, 0)),
            pl.BlockSpec((1, d), lambda i: (0, 0)),
            pl.BlockSpec((None, 1, d), per_batch),
            pl.BlockSpec((None, 1, d), per_batch),
        ],
        out_specs=(pl.BlockSpec((tm, d), lambda i: (i, 0)), pl.BlockSpec((tm, d), lambda i: (i, 0))),
        compiler_params=_params(("parallel",)),
        name="layernorm_modulate",
    )(y, g.reshape(1, d), b.reshape(1, d), sc, sh)


def _norm_mix(o_a, o_b, g_a, g_b, seq):
    rows, wa = o_a.shape
    wb = o_b.shape[1]
    tm = _row_tile(rows, seq)
    return pl.pallas_call(
        _norm_mix_kernel,
        out_shape=jax.ShapeDtypeStruct((rows, wa + wb), BF16),
        grid=(rows // tm,),
        in_specs=[
            pl.BlockSpec((tm, wa), lambda i: (i, 0)),
            pl.BlockSpec((tm, wb), lambda i: (i, 0)),
            pl.BlockSpec((1, wa), lambda i: (0, 0)),
            pl.BlockSpec((1, wb), lambda i: (0, 0)),
        ],
        out_specs=pl.BlockSpec((tm, wa + wb), lambda i: (i, 0)),
        compiler_params=_params(("parallel",)),
        name="norm_mix",
    )(o_a, o_b, g_a.reshape(1, wa), g_b.reshape(1, wb))


def _accumulate(a_ref, w_ref, acc_ref, nk, epilogue):
    def part():
        return jnp.dot(a_ref[...], w_ref[...], preferred_element_type=F32)

    if nk == 1:
        epilogue(part())
        return
    k = pl.program_id(2)

    @pl.when(k == 0)
    def _():
        acc_ref[...] = part()

    @pl.when(jnp.logical_and(k > 0, k < nk - 1))
    def _():
        acc_ref[...] += part()

    @pl.when(k == nk - 1)
    def _():
        epilogue(acc_ref[...] + part())


def _rotate_pairs(y, cos, sin):
    return y * cos + pltpu.roll(y, HEAD_DIM // 2, 1) * sin


def _gemm_relu2_kernel(a_ref, w_ref, o_ref, *scratch, nk):
    def epilogue(acc):
        o_ref[...] = jnp.square(jnp.maximum(acc, 0.0)).astype(o_ref.dtype)
    _accumulate(a_ref, w_ref, scratch[0] if scratch else None, nk, epilogue)


def _gemm_resid_kernel(a_ref, w_ref, x_ref, gate_ref, o_ref, *scratch, nk, alpha):
    def epilogue(acc):
        o_ref[...] = alpha * x_ref[...] + gate_ref[...] * acc
    _accumulate(a_ref, w_ref, scratch[0] if scratch else None, nk, epilogue)


def _gemm_proj_a_kernel(a_ref, w_ref, cos_ref, sin_ref, o_ref, *scratch, nk, rope_tiles):
    heads = o_ref.shape[0]

    def epilogue(acc):
        j = pl.program_id(1)

        @pl.when(j < rope_tiles)
        def _():
            for hh in range(heads):
                y = acc[:, hh * HEAD_DIM:(hh + 1) * HEAD_DIM]
                o_ref[hh] = _rotate_pairs(y, cos_ref[...], sin_ref[...])

        @pl.when(j >= rope_tiles)
        def _():
            for hh in range(heads):
                o_ref[hh] = acc[:, hh * HEAD_DIM:(hh + 1) * HEAD_DIM]

    _accumulate(a_ref, w_ref, scratch[0] if scratch else None, nk, epilogue)


def _axial_normed_rotated(y, g_ref, cos_ref, sin_ref):
    return _rotate_pairs(_rms_rows(y, g_ref[...]), cos_ref[...], sin_ref[...])


def _gemm_proj_bq_kernel(a_ref, w_ref, cos_ref, sin_ref, gq_ref, o_ref, *scratch, nk):
    def epilogue(acc):
        for hh in range(o_ref.shape[0]):
            y = acc[:, hh * HEAD_DIM:(hh + 1) * HEAD_DIM]
            q = _axial_normed_rotated(y, gq_ref, cos_ref, sin_ref)
            o_ref[hh] = (q * (ATTN_SCALE * LOG2_E)).astype(o_ref.dtype)

    _accumulate(a_ref, w_ref, scratch[0] if scratch else None, nk, epilogue)


def _gemm_proj_bkv_kernel(a_ref, w_ref, cos_ref, sin_ref, gk_ref, k_ref, vt_ref, *scratch, nk):
    def epilogue(acc):
        for hh in range(N_KV_B):
            y = acc[:, hh * HEAD_DIM:(hh + 1) * HEAD_DIM]
            k_ref[hh] = _axial_normed_rotated(y, gk_ref, cos_ref, sin_ref).astype(k_ref.dtype)
        for hh in range(N_KV_B):
            y = acc[:, (N_KV_B + hh) * HEAD_DIM:(N_KV_B + hh + 1) * HEAD_DIM]
            vt_ref[hh] = y.T.astype(vt_ref.dtype)

    _accumulate(a_ref, w_ref, scratch[0] if scratch else None, nk, epilogue)


def _gemm_tiles(m, n, k, seq):
    tm = min(1024, seq, m)
    tn = min(1024, n)
    tk = k if k <= 4096 else 2048
    return tm, tn, tk


def _gemm_call(kernel_fn, a, w, extra_inputs, extra_specs, out_shape, out_spec, tiles, n_cols, col_offset, name):
    w_all, layer = w
    m, k = a.shape
    tm, tn, tk = tiles
    nk = k // tk
    joff = col_offset // tn
    scratch = [pltpu.VMEM((tm, tn), F32)] if nk > 1 else []
    return pl.pallas_call(
        functools.partial(kernel_fn, nk=nk),
        out_shape=out_shape,
        grid=(m // tm, n_cols // tn, nk),
        in_specs=[
            pl.BlockSpec((tm, tk), lambda i, j, kk: (i, kk)),
            pl.BlockSpec((None, tk, tn), lambda i, j, kk: (layer, kk, j + joff)),
        ] + extra_specs,
        out_specs=out_spec,
        scratch_shapes=scratch,
        compiler_params=_params(("parallel", "parallel", "arbitrary")),
        name=name,
    )(a, w_all, *extra_inputs)


def _gemm_relu2(a, w):
    m, k = a.shape
    n = w[0].shape[2]
    tiles = _gemm_tiles(m, n, k, m)
    tm, tn, _ = tiles
    return _gemm_call(
        _gemm_relu2_kernel, a, w, [], [],
        jax.ShapeDtypeStruct((m, n), BF16),
        pl.BlockSpec((tm, tn), lambda i, j, kk: (i, j)),
        tiles, n, 0, "gemm_relu2")


def _gemm_resid(a, w, x, gate, seq, alpha):
    m, k = a.shape
    n = w[0].shape[2]
    tiles = _gemm_tiles(m, n, k, seq)
    tm, tn, _ = tiles
    return _gemm_call(
        functools.partial(_gemm_resid_kernel, alpha=alpha), a, w, [x, gate],
        [pl.BlockSpec((tm, tn), lambda i, j, kk: (i, j)),
         pl.BlockSpec((None, 1, tn), lambda i, j, kk: ((i * tm) // seq, 0, j))],
        jax.ShapeDtypeStruct((m, n), F32),
        pl.BlockSpec((tm, tn), lambda i, j, kk: (i, j)),
        tiles, n, 0, "gemm_resid")


def _rope_specs(tm, seq):
    nrow = seq // tm
    return [pl.BlockSpec((tm, LANES), lambda i, j, kk: (i % nrow, 0))] * 2


def _gemm_proj_a(h, w_in, tables, seq):
    m, k = h.shape
    tiles = _gemm_tiles(m, PROJ_A, k, seq)
    tm, tn, _ = tiles
    heads = tn // HEAD_DIM
    return _gemm_call(
        functools.partial(_gemm_proj_a_kernel, rope_tiles=2 * WIDTH_A // tn), h, w_in, list(tables),
        _rope_specs(tm, seq),
        jax.ShapeDtypeStruct((PROJ_A // HEAD_DIM, m, HEAD_DIM), F32),
        pl.BlockSpec((heads, tm, HEAD_DIM), lambda i, j, kk: (j, i, 0)),
        tiles, PROJ_A, 0, "gemm_proj_a")


def _gemm_proj_bq(h, w_in, tables, g_q, seq):
    m, k = h.shape
    tiles = _gemm_tiles(m, WIDTH_B, k, seq)
    tm, tn, _ = tiles
    heads = tn // HEAD_DIM
    return _gemm_call(
        _gemm_proj_bq_kernel, h, w_in, list(tables) + [g_q.reshape(1, HEAD_DIM)],
        _rope_specs(tm, seq) + [pl.BlockSpec((1, HEAD_DIM), lambda i, j, kk: (0, 0))],
        jax.ShapeDtypeStruct((N_HEADS_B, m, HEAD_DIM), BF16),
        pl.BlockSpec((heads, tm, HEAD_DIM), lambda i, j, kk: (j, i, 0)),
        tiles, WIDTH_B, PROJ_A, "gemm_proj_bq")


def _gemm_proj_bkv(h, w_in, tables, g_k, seq):
    m, k = h.shape
    tm, _, tk = _gemm_tiles(m, 2 * KV_WIDTH_B, k, seq)
    tiles = (tm, 2 * KV_WIDTH_B, tk)
    return _gemm_call(
        _gemm_proj_bkv_kernel, h, w_in, list(tables) + [g_k.reshape(1, HEAD_DIM)],
        _rope_specs(tm, seq) + [pl.BlockSpec((1, HEAD_DIM), lambda i, j, kk: (0, 0))],
        (jax.ShapeDtypeStruct((N_KV_B, m, HEAD_DIM), BF16), jax.ShapeDtypeStruct((N_KV_B, HEAD_DIM, m), BF16)),
        (pl.BlockSpec((N_KV_B, tm, HEAD_DIM), lambda i, j, kk: (0, i, 0)),
         pl.BlockSpec((N_KV_B, HEAD_DIM, tm), lambda i, j, kk: (0, 0, i))),
        tiles, 2 * KV_WIDTH_B, PROJ_A + WIDTH_B, "gemm_proj_bkv")


def _mixer_a_kernel(q_ref, k_ref, v_ref, o_ref, acc_ref, m_ref, d_ref, *, seq, window, plan):
    win = pl.program_id(2)

    for branch, (dil, tq) in enumerate(plan):
        length = seq // dil
        tkv = tq + 2 * HALF_WINDOW
        blocks = window // (dil * tq)
        offset = (lax.broadcasted_iota(jnp.int32, (tq, tkv), 1)
                  - lax.broadcasted_iota(jnp.int32, (tq, tkv), 0))

        def rows(start, size, dil=dil):
            return pl.ds(start, size) if dil == 1 else pl.ds(start, size, stride=dil)

        for res in range(dil):
            for blk in range(blocks):
                loc = res + dil * tq * blk
                q0 = win * (window // dil) + tq * blk
                k0 = jnp.clip(q0 - HALF_WINDOW, 0, length - tkv)
                q = (q_ref[rows(loc, tq), :] * (ATTN_SCALE * LOG2_E)).astype(BF16)
                k = k_ref[rows(res + dil * k0, tkv), :].astype(BF16)
                v = v_ref[rows(res + dil * k0, tkv), :].astype(BF16)
                s = lax.dot_general(q, k, (((1,), (1,)), ((), ())), preferred_element_type=F32)
                shifted = offset + (k0 - q0 + HALF_WINDOW)
                valid = lax.bitcast_convert_type(shifted, jnp.uint32) <= 2 * HALF_WINDOW
                s = jnp.where(valid, s, NEG_BIG)
                m = jnp.max(s, axis=-1, keepdims=True)
                p = jnp.exp2(s - m).astype(BF16)
                v_ones = jnp.concatenate([v, jnp.ones((tkv, LANES), BF16)], axis=1)
                pv = jnp.dot(p, v_ones, preferred_element_type=F32)
                acc_ref[branch, rows(loc, tq), :] = pv[:, :HEAD_DIM]
                m_ref[branch, rows(loc, tq), :] = jnp.broadcast_to(m, (tq, LANES))
                d_ref[branch, rows(loc, tq), :] = pv[:, HEAD_DIM:]

    m_all = m_ref[0]
    for branch in range(1, len(plan)):
        m_all = jnp.maximum(m_all, m_ref[branch])
    num = jnp.zeros(o_ref.shape, F32)
    tot = jnp.zeros(o_ref.shape, F32)
    for branch in range(len(plan)):
        w = jnp.exp2(m_ref[branch] - m_all)
        num = num + w * acc_ref[branch]
        tot = tot + w * d_ref[branch]
    o_ref[...] = num / tot


def _mixer_a_plan(seq, window):
    plan = []
    for dil in DILATIONS:
        length = seq // dil
        tq = min(128, length // 2, window // dil)
        assert length >= tq + 2 * HALF_WINDOW and window % (dil * tq) == 0, (seq, window, dil)
        plan.append((dil, tq))
    return tuple(plan)


def _mixer_a(qkv, batch, seq):
    rows = batch * seq
    window = min(2048, seq)
    nwin = seq // window
    plan = _mixer_a_plan(seq, window)
    q_spec = pl.BlockSpec((None, window, HEAD_DIM), lambda b, h, w: (h, b * nwin + w, 0))
    k_spec = pl.BlockSpec((None, seq, HEAD_DIM), lambda b, h, w: (N_HEADS_A + h, b, 0))
    v_spec = pl.BlockSpec((None, seq, HEAD_DIM), lambda b, h, w: (2 * N_HEADS_A + h, b, 0))
    stat = pltpu.VMEM((len(plan), window, LANES), F32)
    return pl.pallas_call(
        functools.partial(_mixer_a_kernel, seq=seq, window=window, plan=plan),
        out_shape=jax.ShapeDtypeStruct((rows, WIDTH_A), F32),
        grid=(batch, N_HEADS_A, nwin),
        in_specs=[q_spec, k_spec, v_spec],
        out_specs=pl.BlockSpec((window, HEAD_DIM), lambda b, h, w: (b * nwin + w, h)),
        scratch_shapes=[stat, stat, stat],
        compiler_params=_params(("parallel", "parallel", "arbitrary")),
        name="mixer_a_dilated",
    )(qkv, qkv, qkv)


SUBLANES = 8
BF16_SUBLANES = 16
MIXER_B_CHAIN_COLS = 512


def _reduce_rows(x, op):
    parts = [x[r:r + SUBLANES] for r in range(0, x.shape[0], SUBLANES)]
    while len(parts) > 1:
        parts = [op(parts[i], parts[i + 1]) for i in range(0, len(parts) - 1, 2)] + (
            [parts[-1]] if len(parts) % 2 else [])
    top = parts[0]
    rows = [top[r:r + 1] for r in range(top.shape[0])]
    while len(rows) > 1:
        rows = [op(rows[i], rows[i + 1]) for i in range(0, len(rows), 2)]
    return rows[0]


def _mixer_b_kernel(q_ref, k_ref, vt_ref, o_ref, m_ref, acc_ref, *, nkv, chains):
    ki = pl.program_id(3)

    @pl.when(ki == 0)
    def _():
        m_ref[...] = jnp.full_like(m_ref, NEG_BIG)
        acc_ref[...] = jnp.zeros_like(acc_ref)

    k = k_ref[...]
    ones = jnp.ones((acc_ref.shape[1] - HEAD_DIM, k.shape[0]), BF16)
    vt = jnp.concatenate([vt_ref[...], ones], axis=0)
    tq = q_ref.shape[1]
    cols = m_ref.shape[-1]
    chains_per_head = tq // cols

    def scores_and_max(c):
        q = q_ref[c // chains_per_head, pl.ds((c % chains_per_head) * cols, cols), :]
        st = lax.dot_general(k, q, (((1,), (1,)), ((), ())), preferred_element_type=F32)
        return st, jnp.maximum(m_ref[c], _reduce_rows(st, jnp.maximum))

    def accumulate(c, st, m_new):
        alpha = jnp.exp2(m_ref[c] - m_new)
        p = jnp.exp2(st - m_new).astype(BF16)
        acc_ref[c] = alpha * acc_ref[c] + jnp.dot(vt, p, preferred_element_type=F32)
        m_ref[c] = m_new

    pending = [scores_and_max(c) for c in range(chains)]
    for c in range(chains):
        accumulate(c, *pending[c])

    @pl.when(ki == nkv - 1)
    def _():
        for c in range(chains):
            head, part = divmod(c, chains_per_head)
            acc = acc_ref[c]
            o_ref[part * cols:(part + 1) * cols, head * HEAD_DIM:(head + 1) * HEAD_DIM] = (
                acc[:HEAD_DIM] / acc[HEAD_DIM:HEAD_DIM + 1]).T


def _mixer_b(q, k, vt, batch, seq):
    rows = batch * seq
    tq = min(1024, seq)
    tkv = min(2048, seq)
    nq, nkv = seq // tq, seq // tkv
    cols = min(MIXER_B_CHAIN_COLS, tq)
    chains = GQA_GROUP * tq // cols
    q_spec = pl.BlockSpec((GQA_GROUP, tq, HEAD_DIM), lambda b, g, qi, ki: (g, b * nq + qi, 0))
    k_spec = pl.BlockSpec((None, tkv, HEAD_DIM), lambda b, g, qi, ki: (g, b * nkv + ki, 0))
    vt_spec = pl.BlockSpec((None, HEAD_DIM, tkv), lambda b, g, qi, ki: (g, 0, b * nkv + ki))
    return pl.pallas_call(
        functools.partial(_mixer_b_kernel, nkv=nkv, chains=chains),
        out_shape=jax.ShapeDtypeStruct((rows, WIDTH_B), F32),
        grid=(batch, N_KV_B, nq, nkv),
        in_specs=[q_spec, k_spec, vt_spec],
        out_specs=pl.BlockSpec((tq, GQA_GROUP * HEAD_DIM), lambda b, g, qi, ki: (b * nq + qi, g)),
        scratch_shapes=[
            pltpu.VMEM((chains, 1, cols), F32),
            pltpu.VMEM((chains, HEAD_DIM + BF16_SUBLANES, cols), F32),
        ],
        compiler_params=_params(("parallel", "parallel", "parallel", "arbitrary")),
        name="mixer_b_gqa",
    )(q, k, vt)


def _paired_layout(rotary_halves):
    lower = [d for a, _, w in rotary_halves for d in range(a, a + w)]
    upper = [d for _, b, w in rotary_halves for d in range(b, b + w)]
    rest = [d for d in range(HEAD_DIM) if d not in lower and d not in upper]
    lower += rest[:len(rest) // 2]
    upper += rest[len(rest) // 2:]
    return np.array(lower + upper)


PERM_A = _paired_layout([(0, PARTIAL_ROT_DIM // 2, PARTIAL_ROT_DIM // 2)])
PERM_B = _paired_layout([(0, HEAD_DIM // 4, HEAD_DIM // 4), (HEAD_DIM // 2, 3 * HEAD_DIM // 4, HEAD_DIM // 4)])


def _rotary_tables(angle_groups):
    seq = angle_groups[0].shape[0]
    cos = np.ones((seq, HEAD_DIM), np.float64)
    sin = np.zeros((seq, HEAD_DIM), np.float64)
    lane = 0
    for ang in angle_groups:
        w = ang.shape[1]
        for base, sign in ((lane, -1.0), (lane + HEAD_DIM // 2, 1.0)):
            cos[:, base:base + w] = np.cos(ang)
            sin[:, base:base + w] = sign * np.sin(ang)
        lane += w
    return tuple(jnp.asarray(t, F32) for t in (cos, sin))


def _inv_freq(rot_dim, theta):
    return theta ** (-np.arange(0, rot_dim, 2, dtype=np.float64) / rot_dim)


def _partial_rope_tables(seq):
    pos = np.arange(seq, dtype=np.float64)
    return _rotary_tables([pos[:, None] * _inv_freq(PARTIAL_ROT_DIM, ROPE_THETA)[None, :]])


def _axial_rope_tables(seq):
    pos = np.arange(seq)
    inv = _inv_freq(HEAD_DIM // 2, AXIAL_THETA)[None, :]
    row = (pos // GRID_W).astype(np.float64)[:, None]
    col = (pos % GRID_W).astype(np.float64)[:, None]
    return _rotary_tables([row * inv, col * inv])


def _run_group(x, mod, weights, depth):
    batch, seq, d = x.shape
    rows = batch * seq
    alpha = (2 * depth) ** 0.25
    tables_a = _partial_rope_tables(seq)
    tables_b = _axial_rope_tables(seq)
    x = x.reshape(rows, d)

    def chunks(l):
        m = mod[l].reshape(batch, 6, 1, d)
        return [m[:, i] for i in range(6)]

    sh1, sc1, g1, sh2, sc2, g2 = chunks(0)
    h = _modulate(x, sc1, sh1, seq)
    for l in range(depth):
        w = weights[l]
        qkv_a = _gemm_proj_a(h, w["w_in"], tables_a, seq)
        q_b = _gemm_proj_bq(h, w["w_in"], tables_b, w["g_q"], seq)
        k_b, vt_b = _gemm_proj_bkv(h, w["w_in"], tables_b, w["g_k"], seq)
        o_a = _mixer_a(qkv_a, batch, seq)
        o_b = _mixer_b(q_b, k_b, vt_b, batch, seq)
        mixed = _norm_mix(o_a, o_b, w["g_out_a"], w["g_out_b"], seq)
        y = _gemm_resid(mixed, w["w_out"], x, g1, seq, alpha)
        x, h = _layernorm_modulate(y, w["ln1_g"], w["ln1_b"], sc2, sh2, seq)
        f = _gemm_relu2(h, w["w_up"])
        y = _gemm_resid(f, w["w_down"], x, g2, seq, alpha)
        if l + 1 < depth:
            sh1, sc1, g1, sh2, sc2, g2 = chunks(l + 1)
            x, h = _layernorm_modulate(y, w["ln2_g"], w["ln2_b"], sc1, sh1, seq)
        else:
            x = _layernorm(y, w["ln2_g"], w["ln2_b"], seq)
    return x.reshape(batch, seq, d)


def _permute_lanes(y, perm):
    lane = lax.broadcasted_iota(jnp.int32, y.shape, 1)
    shifts = (np.arange(HEAD_DIM) - perm) % HEAD_DIM
    out = y
    for shift in sorted(set(shifts.tolist()) - {0}):
        lanes = np.nonzero(shifts == shift)[0]
        lo, hi = int(lanes[0]), int(lanes[-1])
        assert hi - lo + 1 == lanes.size
        out = jnp.where(jnp.logical_and(lane >= lo, lane <= hi), pltpu.roll(y, shift, 1), out)
    return out


def _prep_w_in_kernel(w_ref, o_ref):
    qk_a_heads = 2 * N_HEADS_A
    b_first = PROJ_A // HEAD_DIM
    qk_b_heads = N_HEADS_B + N_KV_B
    for head in range(o_ref.shape[-1] // HEAD_DIM):
        cols = slice(head * HEAD_DIM, (head + 1) * HEAD_DIM)
        y = w_ref[:, cols]
        if head < qk_a_heads:
            y = _permute_lanes(y, PERM_A)
        elif b_first <= head < b_first + qk_b_heads:
            y = _permute_lanes(y, PERM_B)
        o_ref[:, cols] = y.astype(o_ref.dtype)


def _prep_w_in(w_in):
    depth, d, n = w_in.shape
    tk = min(256, d)
    return pl.pallas_call(
        _prep_w_in_kernel,
        out_shape=jax.ShapeDtypeStruct((depth, d, n), BF16),
        grid=(depth, d // tk),
        in_specs=[pl.BlockSpec((None, tk, n), lambda l, i: (l, i, 0))],
        out_specs=pl.BlockSpec((None, tk, n), lambda l, i: (l, i, 0)),
        compiler_params=_params(("parallel", "parallel")),
        name="prep_w_in",
    )(w_in)


def kernel(x_prompt, x_sample, c_prompt, c_sample, w_ada, b_ada, w_in, g_q, g_k, g_out_a, g_out_b, w_out,
           ln1_g, ln1_b, w_up, w_down, ln2_g, ln2_b):
    depth = w_ada.shape[0]
    n_prompt, n_sample = c_prompt.shape[0], c_sample.shape[0]
    pad = -(n_prompt + n_sample) % 8
    c_rows = jnp.concatenate([c_prompt, c_sample, jnp.zeros((pad, c_prompt.shape[1]), F32)], axis=0)
    mod = _ada(c_rows, w_ada, b_ada)

    w_in = _prep_w_in(w_in)
    g_q, g_k = g_q[:, PERM_B], g_k[:, PERM_B]
    w_out, w_up, w_down = (w.astype(BF16) for w in (w_out, w_up, w_down))
    weights = [
        dict(w_in=(w_in, l), w_out=(w_out, l), w_up=(w_up, l), w_down=(w_down, l),
             g_q=g_q[l], g_k=g_k[l], g_out_a=g_out_a[l], g_out_b=g_out_b[l],
             ln1_g=ln1_g[l], ln1_b=ln1_b[l], ln2_g=ln2_g[l], ln2_b=ln2_b[l])
        for l in range(depth)
    ]
    y_prompt = _run_group(x_prompt, mod[:, :n_prompt], weights, depth)
    y_sample = _run_group(x_sample, mod[:, n_prompt:n_prompt + n_sample], weights, depth)
    return (y_prompt, y_sample)
```
